```python
import math
import jax, jax.numpy as jnp
from jax import lax
import numpy as np

D_MODEL = 1024
BATCH = 16
SEQ = 2048
DEPTH = 2
DEC_BATCH = 128
DEC_SEQ = 4
PAST_LEN = 16384
PAGE_SIZE = 128

MLA_HEADS = 8
QK_NOPE = 64
QK_ROPE = 32
V_HEAD = 64
Q_LORA = 384
KV_LORA = 256
ROPE_THETA = 10000.0
Q_BLOCK = 128
SSM_WIDTH = D_MODEL // 2
SSM_GROUP = 16
SSM_GROUPS = SSM_WIDTH // SSM_GROUP
SSM_STATE = 64
DT_MIN = 0.001
DT_MAX = 0.1
N_EXPERTS = 64
N_EXPERT_GROUPS = 8
TOPK_GROUPS = 4
TOP_K = 6
D_EXPERT = D_MODEL // 4
D_SHARED = D_MODEL // 4
ROUTED_SCALE = 2.5
MOE_BLOCK = 128
RMS_EPS = 1e-6
IN_WIDTH = Q_LORA + KV_LORA + QK_ROPE + SSM_WIDTH + 2 * D_MODEL
IN_SPLITS = (Q_LORA, Q_LORA + KV_LORA, Q_LORA + KV_LORA + QK_ROPE, Q_LORA + KV_LORA + QK_ROPE + SSM_WIDTH, Q_LORA + KV_LORA + QK_ROPE + SSM_WIDTH + D_MODEL)

kernel_name = 'hybrid_mla_s5_moe_adaln_step'


def rms_norm(x, g):
    xf = x.astype(jnp.float32)
    y = xf * lax.rsqrt(jnp.mean(xf * xf, axis=-1, keepdims=True) + RMS_EPS)
    return (y * g.astype(jnp.float32)).astype(x.dtype)


def rope_tables(pos):
    inv = ROPE_THETA ** (-jnp.arange(0, QK_ROPE, 2, dtype=jnp.float32) / QK_ROPE)
    ang = pos.astype(jnp.float32)[:, None] * inv[None, :]
    return jnp.cos(ang), jnp.sin(ang)


def apply_rope(x, cos, sin):
    shape = (x.shape[1],) + (1,) * (x.ndim - 3) + (QK_ROPE // 2,)
    c, s = cos.reshape(shape), sin.reshape(shape)
    xf = x.astype(jnp.float32)
    x1, x2 = xf[..., :QK_ROPE // 2], xf[..., QK_ROPE // 2:]
    return jnp.concatenate([x1 * c - x2 * s, x2 * c + x1 * s], axis=-1).astype(x.dtype)


def latent_attention(q_lat, q_pe, q_pos, segments):
    scale = (QK_NOPE + QK_ROPE) ** -0.5
    B, T = q_lat.shape[:2]
    qb = min(Q_BLOCK, T)
    nb = T // qb

    def block(args):
        ql, qp, qpos = args
        scores = []
        for kv, kpe, kpos in segments:
            s = (jnp.einsum('bqhr,bkr->bhqk', ql, kv, preferred_element_type=jnp.float32)
                 + jnp.einsum('bqhp,bkp->bhqk', qp, kpe, preferred_element_type=jnp.float32)) * scale
            scores.append(jnp.where(kpos[None, :] <= qpos[:, None], s, -jnp.inf))
        probs = jax.nn.softmax(jnp.concatenate(scores, axis=-1), axis=-1)
        out = jnp.zeros(ql.shape, jnp.float32)
        off = 0
        for kv, _, _ in segments:
            n = kv.shape[1]
            out = out + jnp.einsum('bhqk,bkr->bqhr', probs[..., off:off + n].astype(kv.dtype), kv, preferred_element_type=jnp.float32)
            off += n
        return out.astype(ql.dtype)

    to_blocks = lambda a: a.reshape((B, nb, qb) + a.shape[2:]).swapaxes(0, 1)
    out = lax.map(block, (to_blocks(q_lat), to_blocks(q_pe), q_pos.reshape(nb, qb)))
    return out.swapaxes(0, 1).reshape((B, T) + out.shape[3:])


def mla_branch(cq_in, ckv_in, kpe_in, pos, attn_past, p):
    B, T, _ = cq_in.shape
    q = jnp.einsum('btc,chd->bthd', rms_norm(cq_in, p['g_qnorm']), p['w_uq'])
    cos, sin = rope_tables(pos)
    q_pe = apply_rope(q[..., QK_NOPE:], cos, sin)
    q_lat = jnp.einsum('bthn,hrn->bthr', q[..., :QK_NOPE], p['w_uk'])
    kv = rms_norm(ckv_in, p['g_kvnorm'])
    kpe = apply_rope(kpe_in, cos, sin)
    o_lat = latent_attention(q_lat, q_pe, pos, attn_past + ((kv, kpe, pos),))
    o = jnp.einsum('bthr,hrv->bthv', o_lat, p['w_uv']).reshape(B, T, MLA_HEADS * V_HEAD)
    return o, kv, kpe


def s5_branch(u, h0_re, h0_im, p):
    f32 = jnp.float32
    B, T, _ = u.shape
    ug = u.reshape(B, T, SSM_GROUPS, SSM_GROUP).astype(f32)
    a_re, a_im = p['ssm_a_re'].astype(f32), p['ssm_a_im'].astype(f32)
    dt = jnp.exp(p['ssm_log_dt'].astype(f32))[:, None]
    mag = jnp.exp(dt * a_re)
    abar_re, abar_im = mag * jnp.cos(dt * a_im), mag * jnp.sin(dt * a_im)
    den = a_re * a_re + a_im * a_im
    coef_re = ((abar_re - 1.0) * a_re + abar_im * a_im) / den
    coef_im = (abar_im * a_re - (abar_re - 1.0) * a_im) / den
    b_re, b_im = p['ssm_b_re'].astype(f32), p['ssm_b_im'].astype(f32)
    bbar_re = coef_re[..., None] * b_re - coef_im[..., None] * b_im
    bbar_im = coef_re[..., None] * b_im + coef_im[..., None] * b_re
    bu_re = jnp.einsum('gpc,btgc->btgp', bbar_re, ug)
    bu_im = jnp.einsum('gpc,btgc->btgp', bbar_im, ug)
    h0r, h0i = h0_re.astype(f32), h0_im.astype(f32)
    bu_re = bu_re.at[:, 0].add(abar_re * h0r - abar_im * h0i)
    bu_im = bu_im.at[:, 0].add(abar_re * h0i + abar_im * h0r)
    ar = jnp.broadcast_to(abar_re, bu_re.shape)
    ai = jnp.broadcast_to(abar_im, bu_im.shape)

    def combine(e1, e2):
        a1r, a1i, b1r, b1i = e1
        a2r, a2i, b2r, b2i = e2
        return (a2r * a1r - a2i * a1i, a2r * a1i + a2i * a1r,
                a2r * b1r - a2i * b1i + b2r, a2r * b1i + a2i * b1r + b2i)

    _, _, h_re, h_im = lax.associative_scan(combine, (ar, ai, bu_re, bu_im), axis=1)
    y = (jnp.einsum('gcp,btgp->btgc', p['ssm_c_re'].astype(f32), h_re)
         - jnp.einsum('gcp,btgp->btgc', p['ssm_c_im'].astype(f32), h_im)
         + p['ssm_d'].astype(f32) * ug).reshape(B, T, SSM_WIDTH)
    g = jax.nn.gelu(y)
    out = g * jax.nn.sigmoid(g @ p['w_glu'].astype(f32) + p['b_glu'].astype(f32))
    return out.astype(u.dtype), h_re[:, -1], h_im[:, -1]


def token_mixer(h, pos, attn_past, h0_re, h0_im, p):
    z = h @ p['w_in']
    cq, ckv, kpe, u, g_a, g_s = jnp.split(z, IN_SPLITS, axis=-1)
    o_a, kv, kpe_rot = mla_branch(cq, ckv, kpe, pos, attn_past, p)
    o_s, hr, hi = s5_branch(u, h0_re, h0_im, p)
    merged = jax.nn.sigmoid(g_a) * (o_a @ p['w_attn_proj']) + jax.nn.sigmoid(g_s) * (o_s @ p['w_ssm_proj'])
    return merged @ p['w_out'], kv, kpe_rot, hr, hi


def swiglu(x, w1, w3, w2):
    return (jax.nn.silu(x @ w1) * (x @ w3)) @ w2


def routed_experts(xf, idx, wts, w1, w3, w2):
    N, D = xf.shape
    A = N * TOP_K
    flat_e = idx.reshape(A)
    flat_tok = jnp.repeat(jnp.arange(N, dtype=jnp.int32), TOP_K)
    flat_w = wts.reshape(A)
    order = jnp.argsort(flat_e)
    e_sorted, tok_sorted, w_sorted = flat_e[order], flat_tok[order], flat_w[order]
    counts = jnp.bincount(flat_e, length=N_EXPERTS)
    padded = (counts + MOE_BLOCK - 1) // MOE_BLOCK * MOE_BLOCK
    pad_end = jnp.cumsum(padded)
    pad_start = pad_end - padded
    start = jnp.cumsum(counts) - counts
    dest = pad_start[e_sorted] + jnp.arange(A, dtype=jnp.int32) - start[e_sorted]
    n_blocks = -(-A // MOE_BLOCK) + N_EXPERTS
    L = n_blocks * MOE_BLOCK
    buf_tok = jnp.full((L,), N, jnp.int32).at[dest].set(tok_sorted)
    buf_w = jnp.zeros((L,), xf.dtype).at[dest].set(w_sorted)
    block_e = jnp.minimum(jnp.searchsorted(pad_end, jnp.arange(n_blocks, dtype=jnp.int32) * MOE_BLOCK, side='right'), N_EXPERTS - 1)
    x_pad = jnp.concatenate([xf, jnp.zeros((1, D), xf.dtype)], axis=0)

    def expert_block(args):
        tok, e = args
        return swiglu(x_pad[tok], w1[e], w3[e], w2[e])

    out = lax.map(expert_block, (buf_tok.reshape(n_blocks, MOE_BLOCK), block_e))
    out = out.reshape(L, D) * buf_w[:, None]
    return jax.ops.segment_sum(out, buf_tok, num_segments=N + 1)[:N]


def moe_ffn(h, p):
    B, T, D = h.shape
    xf = h.reshape(B * T, D)
    scores = jax.nn.sigmoid((xf @ p['w_router']).astype(jnp.float32))
    choice = scores + p['router_bias'].astype(jnp.float32)
    grouped = choice.reshape(-1, N_EXPERT_GROUPS, N_EXPERTS // N_EXPERT_GROUPS)
    group_score = lax.top_k(grouped, 2)[0].sum(-1)
    top_groups = lax.top_k(group_score, TOPK_GROUPS)[1]
    group_mask = jax.nn.one_hot(top_groups, N_EXPERT_GROUPS).sum(-2) > 0
    expert_mask = jnp.repeat(group_mask, N_EXPERTS // N_EXPERT_GROUPS, axis=-1)
    idx = lax.top_k(jnp.where(expert_mask, choice, -jnp.inf), TOP_K)[1]
    wts = jnp.take_along_axis(scores, idx, axis=-1)
    wts = wts / jnp.sum(wts, axis=-1, keepdims=True) * ROUTED_SCALE
    routed = routed_experts(xf, idx, wts.astype(xf.dtype), p['w_e1'], p['w_e3'], p['w_e2'])
    shared = swiglu(xf, p['w_s1'], p['w_s3'], p['w_s2'])
    return (routed + shared).reshape(B, T, D)


def decoder_layer(x, c, pos, attn_past, h0_re, h0_im, p):
    mod = (jax.nn.silu(c) @ p['w_ada'] + p['b_ada'])[:, None, :]
    sh_a, sc_a, gt_a, sh_m, sc_m, gt_m = jnp.split(mod, 6, axis=-1)
    h = rms_norm(x, p['g_mix']) * (1 + sc_a) + sh_a
    m, kv, kpe, hr, hi = token_mixer(h, pos, attn_past, h0_re, h0_im, p)
    x = x + gt_a * m
    h = rms_norm(x, p['g_ffn']) * (1 + sc_m) + sh_m
    x = x + gt_m * moe_ffn(h, p)
    return x, kv, kpe, hr, hi


def setup_inputs(seed: int = 0) -> dict:
    key = jax.random.key(seed)
    ks = iter(jax.random.split(key, 64))
    f32 = jnp.float32

    def nrm(shape, scale=1.0):
        return jax.random.normal(next(ks), shape, f32) * scale

    def gain(shape):
        return 1.0 + nrm(shape, 0.01)

    n_pages = PAST_LEN // PAGE_SIZE
    n_used = DEC_BATCH * n_pages
    n_pool = n_used + (n_used + 3) // 4
    L, D, H = DEPTH, D_MODEL, MLA_HEADS
    G, P, C = SSM_GROUPS, SSM_STATE, SSM_GROUP
    E, F = N_EXPERTS, D_EXPERT
    return {
        'x_prompt': nrm((BATCH, SEQ, D)),
        'x_sample': nrm((DEC_BATCH, DEC_SEQ, D)),
        'cache_latent': nrm((L, n_pool, PAGE_SIZE, KV_LORA)),
        'cache_krope': nrm((L, n_pool, PAGE_SIZE, QK_ROPE)),
        'state_ssm_re': nrm((L, DEC_BATCH, G, P), 0.5),
        'state_ssm_im': nrm((L, DEC_BATCH, G, P), 0.5),
        'page_table': jax.random.permutation(next(ks), n_pool)[:n_used].reshape(DEC_BATCH, n_pages).astype(jnp.int32),
        'c_prompt': nrm((BATCH, D)),
        'c_sample': nrm((DEC_BATCH, D)),
        'w_ada': nrm((L, D, 6 * D), 0.5 * D ** -0.5),
        'b_ada': nrm((L, 6 * D), 0.01),
        'g_mix': gain((L, D)),
        'w_in': nrm((L, D, IN_WIDTH), D ** -0.5),
        'g_qnorm': gain((L, Q_LORA)),
        'w_uq': nrm((L, Q_LORA, H, QK_NOPE + QK_ROPE), Q_LORA ** -0.5),
        'g_kvnorm': gain((L, KV_LORA)),
        'w_uk': nrm((L, H, KV_LORA, QK_NOPE), KV_LORA ** -0.5),
        'w_uv': nrm((L, H, KV_LORA, V_HEAD), KV_LORA ** -0.5),
        'ssm_a_re': -0.5 + nrm((L, G, P), 0.01),
        'ssm_a_im': jnp.pi * jnp.arange(P, dtype=f32)[None, None, :] + nrm((L, G, P), 0.01),
        'ssm_log_dt': jax.random.uniform(next(ks), (L, G), f32, math.log(DT_MIN), math.log(DT_MAX)),
        'ssm_b_re': nrm((L, G, P, C), (2.0 * C) ** -0.5),
        'ssm_b_im': nrm((L, G, P, C), (2.0 * C) ** -0.5),
        'ssm_c_re': nrm((L, G, C, P), (2.0 * P) ** -0.5),
        'ssm_c_im': nrm((L, G, C, P), (2.0 * P) ** -0.5),
        'ssm_d': nrm((L, G, C)),
        'w_glu': nrm((L, SSM_WIDTH, SSM_WIDTH), SSM_WIDTH ** -0.5),
        'b_glu': nrm((L, SSM_WIDTH), 0.01),
        'w_attn_proj': nrm((L, H * V_HEAD, D), (H * V_HEAD) ** -0.5),
        'w_ssm_proj': nrm((L, SSM_WIDTH, D), SSM_WIDTH ** -0.5),
        'w_out': nrm((L, D, D), D ** -0.5),
        'g_ffn': gain((L, D)),
        'w_router': nrm((L, D, E), D ** -0.5),
        'router_bias': nrm((L, E), 0.01),
        'w_e1': nrm((L, E, D, F), D ** -0.5),
        'w_e3': nrm((L, E, D, F), D ** -0.5),
        'w_e2': nrm((L, E, F, D), F ** -0.5),
        'w_s1': nrm((L, D, D_SHARED), D ** -0.5),
        'w_s3': nrm((L, D, D_SHARED), D ** -0.5),
        'w_s2': nrm((L, D_SHARED, D), D_SHARED ** -0.5),
        'g_final': gain((D,)),
    }


def reference(x_prompt, x_sample, cache_latent, cache_krope, state_ssm_re, state_ssm_im, page_table, c_prompt, c_sample,
              w_ada, b_ada, g_mix, w_in, g_qnorm, w_uq, g_kvnorm, w_uk, w_uv, ssm_a_re, ssm_a_im, ssm_log_dt,
              ssm_b_re, ssm_b_im, ssm_c_re, ssm_c_im, ssm_d, w_glu, b_glu, w_attn_proj, w_ssm_proj, w_out, g_ffn,
              w_router, router_bias, w_e1, w_e3, w_e2, w_s1, w_s3, w_s2, g_final):
    stacked = dict(w_ada=w_ada, b_ada=b_ada, g_mix=g_mix, w_in=w_in, g_qnorm=g_qnorm, w_uq=w_uq, g_kvnorm=g_kvnorm,
                   w_uk=w_uk, w_uv=w_uv, ssm_a_re=ssm_a_re, ssm_a_im=ssm_a_im, ssm_log_dt=ssm_log_dt,
                   ssm_b_re=ssm_b_re, ssm_b_im=ssm_b_im, ssm_c_re=ssm_c_re, ssm_c_im=ssm_c_im, ssm_d=ssm_d,
                   w_glu=w_glu, b_glu=b_glu, w_attn_proj=w_attn_proj, w_ssm_proj=w_ssm_proj, w_out=w_out,
                   g_ffn=g_ffn, w_router=w_router, router_bias=router_bias, w_e1=w_e1, w_e3=w_e3, w_e2=w_e2,
                   w_s1=w_s1, w_s3=w_s3, w_s2=w_s2)
    n_pages = PAST_LEN // PAGE_SIZE
    n_dec = x_sample.shape[0]
    pos_p = jnp.arange(x_prompt.shape[1], dtype=jnp.int32)
    pos_s = PAST_LEN + jnp.arange(x_sample.shape[1], dtype=jnp.int32)
    pos_past = jnp.arange(n_pages * PAGE_SIZE, dtype=jnp.int32)
    h0_p = jnp.zeros((x_prompt.shape[0], SSM_GROUPS, SSM_STATE), jnp.float32)
    xp, xs = x_prompt, x_sample
    lat_p, kpe_p, sre_p, sim_p, lat_s, kpe_s, sre_s, sim_s = ([] for _ in range(8))
    for l in range(DEPTH):
        p = {name: arr[l] for name, arr in stacked.items()}
        xp, kv, kpe, hr, hi = decoder_layer(xp, c_prompt, pos_p, (), h0_p, h0_p, p)
        lat_p.append(kv)
        kpe_p.append(kpe)
        sre_p.append(hr)
        sim_p.append(hi)
        kv_past = cache_latent[l, page_table].reshape(n_dec, n_pages * PAGE_SIZE, KV_LORA)
        kpe_past = cache_krope[l, page_table].reshape(n_dec, n_pages * PAGE_SIZE, QK_ROPE)
        xs, kv, kpe, hr, hi = decoder_layer(xs, c_sample, pos_s, ((kv_past, kpe_past, pos_past),),
                                            state_ssm_re[l], state_ssm_im[l], p)
        lat_s.append(kv)
        kpe_s.append(kpe)
        sre_s.append(hr)
        sim_s.append(hi)
    y_prompt = rms_norm(xp, g_final)
    y_sample = rms_norm(xs, g_final)
    return (y_prompt, y_sample, jnp.stack(lat_p), jnp.stack(kpe_p), jnp.stack(sre_p), jnp.stack(sim_p),
            jnp.stack(lat_s), jnp.stack(kpe_s), jnp.stack(sre_s), jnp.stack(sim_s))
```

```python
import functools
import math

import jax
import jax.numpy as jnp
import numpy as np
from jax import lax
from jax.experimental import pallas as pl
from jax.experimental.pallas import tpu as pltpu

F32 = jnp.float32
BF16 = jnp.bfloat16

ROPE_THETA = 10000.0
RMS_EPS = 1e-6
N_EXPERT_GROUPS = 8
TOPK_GROUPS = 4
TOP_K = 6
ROUTED_SCALE = 2.5
SSM_GROUP = 16

LANES = 128
SUBLANES = 8
VMEM_LIMIT_BYTES = 56 * 1024 * 1024

NEG_BIG = -1e30
TOP_K_PAD = 8
MOE_ROWS = 256
PAGES_PER_STEP = 16


def _cparams(*sem):
    return pltpu.CompilerParams(dimension_semantics=sem, vmem_limit_bytes=VMEM_LIMIT_BYTES)


def _dot(a, b):
    return jnp.dot(a, b, preferred_element_type=F32)


def _dot_nt(a, b):
    return lax.dot_general(a, b, (((1,), (1,)), ((), ())), preferred_element_type=F32)


def _rms(x, g):
    return x * lax.rsqrt(jnp.mean(x * x, axis=-1, keepdims=True) + RMS_EPS) * g


def _mod_kernel(c_ref, w_ref, b_ref, o_ref):
    c = c_ref[...]
    a = (c * jax.nn.sigmoid(c)).astype(BF16)
    o_ref[...] = _dot(a, w_ref[...]) + b_ref[...]


def _modulation(c_all, w_ada, b_ada):
    L, D, D6 = w_ada.shape
    R = c_all.shape[0]
    return pl.pallas_call(
        _mod_kernel,
        grid=(L, D6 // D),
        in_specs=[pl.BlockSpec((R, D), lambda l, j: (0, 0)),
                  pl.BlockSpec((None, D, D), lambda l, j: (l, 0, j)),
                  pl.BlockSpec((None, 1, D), lambda l, j: (l, 0, j))],
        out_specs=pl.BlockSpec((None, R, D), lambda l, j: (l, 0, j)),
        out_shape=jax.ShapeDtypeStruct((L, R, D6), F32),
        compiler_params=_cparams("arbitrary", "arbitrary"),
        name="adaln_mod",
    )(c_all, w_ada, b_ada)


def _in_proj_kernel(dims, x_ref, sh_ref, sc_ref, gmix_ref, wa_ref, wg_ref, gq_ref, wq_ref, wuk_ref,
                    sel_ref, gkv_ref, c8_ref, s8_ref, ct_ref, st_ref,
                    q_ref, kcat_ref, kv_ref, kpe_ref, u_ref, ga_ref, gs_ref):
    H, QL, KVL, ROPE, NOPE, SW, D = dims
    half = H * ROPE // 2
    x = x_ref[...]
    h = _rms(x, gmix_ref[...]) * (1.0 + sc_ref[...]) + sh_ref[...]
    hb = h.astype(BF16)
    za = _dot(hb, wa_ref[...])
    zg = _dot(hb, wg_ref[...])
    ga_ref[...] = jax.nn.sigmoid(zg[:, :D])
    gs_ref[...] = jax.nn.sigmoid(zg[:, D:])
    o0 = QL
    o1 = o0 + KVL
    o2 = o1 + LANES
    o3 = o2 + LANES
    u_ref[...] = za[:, o3:o3 + SW]
    kv = _rms(za[:, o0:o1], gkv_ref[...])
    kv_ref[...] = kv
    krot = za[:, o1:o2] * ct_ref[...] + za[:, o2:o3] * st_ref[...]
    kpe_ref[...] = krot[:, :ROPE]
    kcat_ref[:, :KVL] = kv.astype(BF16)
    kcat_ref[:, KVL:] = krot.astype(BF16)
    cqn = _rms(za[:, :QL], gq_ref[...]).astype(BF16)
    qq = _dot(cqn, wq_ref[...])
    qn = qq[:, :H * NOPE]
    x1 = qq[:, H * NOPE:H * NOPE + half]
    x2 = qq[:, H * NOPE + half:]
    c8 = c8_ref[...]
    s8 = s8_ref[...]
    rot = jnp.concatenate([x1 * c8 - x2 * s8, x2 * c8 + x1 * s8], axis=-1).astype(BF16)
    qpe = _dot(rot, sel_ref[...])
    heads_per_slab = LANES // NOPE
    for p in range(H // heads_per_slab):
        ql = _dot(qn[:, p * LANES:(p + 1) * LANES].astype(BF16), wuk_ref[p])
        for j in range(heads_per_slab):
            hh = p * heads_per_slab + j
            q_ref[hh, :, :KVL] = ql[:, j * KVL:(j + 1) * KVL].astype(BF16)
    for hh in range(H):
        q_ref[hh, :, KVL:] = qpe[:, hh * LANES:(hh + 1) * LANES].astype(BF16)


def _in_proj(x, mod, per_row_mod, p, tabs, dims, tm):
    H, QL, KVL, ROPE, NOPE, SW, D = dims
    NB, TT, _ = x.shape
    KC = KVL + LANES
    grid = (NB, TT // tm)
    if per_row_mod:
        mod_spec = lambda k: pl.BlockSpec((None, tm, D), lambda b, i, k=k: (b, i, k))
    else:
        mod_spec = lambda k: pl.BlockSpec((None, 1, D), lambda b, i, k=k: (b, 0, k))
    full = lambda a: pl.BlockSpec(a.shape, lambda b, i, n=a.ndim: (0,) * n)
    tab = lambda a: pl.BlockSpec((tm, a.shape[1]), lambda b, i: (i, 0))
    row = lambda w: pl.BlockSpec((None, tm, w), lambda b, i: (b, i, 0))
    c8, s8, ct, st = tabs
    weights = [p["g_mix"], p["w_a"], p["w_g"], p["g_qnorm"], p["w_q"], p["w_ukp"], p["sel"], p["g_kvnorm"]]
    in_specs = ([row(D), mod_spec(0), mod_spec(1)] + [full(w) for w in weights]
                + [tab(c8), tab(s8), tab(ct), tab(st)])
    out_shape = [jax.ShapeDtypeStruct((NB, H, TT, KC), BF16),
                 jax.ShapeDtypeStruct((NB, TT, KC), BF16),
                 jax.ShapeDtypeStruct((NB, TT, KVL), F32),
                 jax.ShapeDtypeStruct((NB, TT, ROPE), F32),
                 jax.ShapeDtypeStruct((NB, TT, SW), F32),
                 jax.ShapeDtypeStruct((NB, TT, D), F32),
                 jax.ShapeDtypeStruct((NB, TT, D), F32)]
    out_specs = [pl.BlockSpec((None, H, tm, KC), lambda b, i: (b, 0, i, 0)),
                 row(KC), row(KVL), row(ROPE), row(SW), row(D), row(D)]
    return pl.pallas_call(
        functools.partial(_in_proj_kernel, dims),
        grid=grid, in_specs=in_specs, out_specs=out_specs, out_shape=out_shape,
        compiler_params=_cparams("arbitrary", "arbitrary"),
        name="in_proj",
    )(x, mod, mod, *weights, c8, s8, ct, st)


def _attn_prompt_kernel(dims, qi_ref, kj_ref, q_ref, k_ref, o_ref, m_ref, l_ref, acc_ref):
    H, KVL, tq, scale = dims
    s_idx = pl.program_id(1)
    qi = qi_ref[s_idx]
    kj = kj_ref[s_idx]
    M = H * tq

    @pl.when(kj == 0)
    def _():
        m_ref[...] = jnp.full(m_ref.shape, NEG_BIG, F32)
        l_ref[...] = jnp.zeros(l_ref.shape, F32)
        acc_ref[...] = jnp.zeros(acc_ref.shape, F32)

    q = q_ref[...].reshape(M, q_ref.shape[-1])
    k = k_ref[...]
    s = _dot_nt(q, k) * scale

    def update(s):
        m_prev = m_ref[...]
        m_new = jnp.maximum(m_prev, jnp.max(s, axis=-1, keepdims=True))
        alpha = jnp.exp(m_prev - m_new)
        pr = jnp.exp(s - m_new)
        l_ref[...] = alpha * l_ref[...] + jnp.sum(pr, axis=-1, keepdims=True)
        acc_ref[...] = alpha * acc_ref[...] + _dot(pr.astype(BF16), k[:, :KVL])
        m_ref[...] = m_new

    @pl.when(kj < qi)
    def _():
        update(s)

    @pl.when(kj == qi)
    def _():
        tk = k.shape[0]
        t_of_row = lax.broadcasted_iota(jnp.int32, (H, tq, tk), 1).reshape(M, tk)
        key = lax.broadcasted_iota(jnp.int32, (M, tk), 1)
        update(jnp.where(key <= t_of_row, s, NEG_BIG))
        out = acc_ref[...] / l_ref[...]
        for hh in range(H):
            o_ref[:, hh * KVL:(hh + 1) * KVL] = out[hh * tq:(hh + 1) * tq].astype(o_ref.dtype)


def _attn_prompt(q, kcat, KVL, scale, tq):
    NB, H, TT, KC = q.shape
    nq = TT // tq
    pairs = [(i, j) for i in range(nq) for j in range(i + 1)]
    qi = jnp.asarray([a for a, _ in pairs], jnp.int32)
    kj = jnp.asarray([b for _, b in pairs], jnp.int32)
    M = H * tq
    grid_spec = pltpu.PrefetchScalarGridSpec(
        num_scalar_prefetch=2,
        grid=(NB, len(pairs)),
        in_specs=[pl.BlockSpec((None, H, tq, KC), lambda b, s, qi, kj: (b, 0, qi[s], 0)),
                  pl.BlockSpec((None, tq, KC), lambda b, s, qi, kj: (b, kj[s], 0))],
        out_specs=pl.BlockSpec((None, tq, H * KVL), lambda b, s, qi, kj: (b, qi[s], 0)),
        scratch_shapes=[pltpu.VMEM((M, 1), F32), pltpu.VMEM((M, 1), F32), pltpu.VMEM((M, KVL), F32)],
    )
    return pl.pallas_call(
        functools.partial(_attn_prompt_kernel, (H, KVL, tq, scale)),
        grid_spec=grid_spec,
        out_shape=jax.ShapeDtypeStruct((NB, TT, H * KVL), BF16),
        compiler_params=_cparams("arbitrary", "arbitrary"),
        name="attn_prompt",
    )(qi, kj, q, kcat)


def _attn_sample_kernel(dims, pt_ref, q_ref, knew_ref, *rest):
    H, KVL, ROPE, TD, PG, PAGE, scale = dims
    lat_refs = rest[:PG]
    kr_refs = rest[PG:2 * PG]
    o_ref, m_ref, l_ref, acc_ref, kb_ref, krb_ref = rest[2 * PG:]
    step = pl.program_id(1)
    M = H * TD

    @pl.when(step == 0)
    def _():
        m_ref[...] = jnp.full(m_ref.shape, NEG_BIG, F32)
        l_ref[...] = jnp.zeros(l_ref.shape, F32)
        acc_ref[...] = jnp.zeros(acc_ref.shape, F32)

    q = q_ref[...]
    ql = q[:, :KVL]
    qp = q[:, KVL:KVL + ROPE]

    def update(s, v):
        m_prev = m_ref[...]
        m_new = jnp.maximum(m_prev, jnp.max(s, axis=-1, keepdims=True))
        alpha = jnp.exp(m_prev - m_new)
        pr = jnp.exp(s - m_new)
        l_ref[...] = alpha * l_ref[...] + jnp.sum(pr, axis=-1, keepdims=True)
        acc_ref[...] = alpha * acc_ref[...] + _dot(pr.astype(BF16), v)
        m_ref[...] = m_new

    for j in range(PG):
        kb_ref[j * PAGE:(j + 1) * PAGE, :] = lat_refs[j][...].astype(BF16)
        krb_ref[j * PAGE:(j + 1) * PAGE, :] = kr_refs[j][...].astype(BF16)
    kb = kb_ref[...]
    s = (_dot_nt(ql, kb) + _dot_nt(qp, krb_ref[...])) * scale
    update(s, kb)

    @pl.when(step == pl.num_programs(1) - 1)
    def _():
        kn = knew_ref[...]
        KN = kn.shape[0]
        sn = _dot_nt(q, kn) * scale
        t_of_row = lax.broadcasted_iota(jnp.int32, (H, TD, KN), 1).reshape(M, KN)
        key = lax.broadcasted_iota(jnp.int32, (M, KN), 1)
        update(jnp.where(key <= t_of_row, sn, NEG_BIG), kn[:, :KVL])
        o_ref[...] = (acc_ref[...] / l_ref[...]).astype(o_ref.dtype)


def _attn_sample(q, knew, cache_latent, cache_krope, page_table, layer, scale, H, TD):
    NB, M, KC = q.shape
    _, _, PAGE, KVL = cache_latent.shape
    ROPE = cache_krope.shape[-1]
    n_pages = page_table.shape[1]
    PG = PAGES_PER_STEP
    assert n_pages % PG == 0
    KN = knew.shape[1]

    def page_spec(width, j):
        return pl.BlockSpec((None, None, PAGE, width),
                            lambda b, s, pt, j=j: (layer, pt[b, s * PG + j], 0, 0))

    in_specs = ([pl.BlockSpec((None, M, KC), lambda b, s, pt: (b, 0, 0)),
                 pl.BlockSpec((None, KN, KC), lambda b, s, pt: (b, 0, 0))]
                + [page_spec(KVL, j) for j in range(PG)]
                + [page_spec(ROPE, j) for j in range(PG)])
    grid_spec = pltpu.PrefetchScalarGridSpec(
        num_scalar_prefetch=1, grid=(NB, n_pages // PG), in_specs=in_specs,
        out_specs=pl.BlockSpec((None, M, KVL), lambda b, s, pt: (b, 0, 0)),
        scratch_shapes=[pltpu.VMEM((M, 1), F32), pltpu.VMEM((M, 1), F32), pltpu.VMEM((M, KVL), F32),
                        pltpu.VMEM((PG * PAGE, KVL), BF16), pltpu.VMEM((PG * PAGE, ROPE), BF16)],
    )
    return pl.pallas_call(
        functools.partial(_attn_sample_kernel, (H, KVL, ROPE, TD, PG, PAGE, scale)),
        grid_spec=grid_spec,
        out_shape=jax.ShapeDtypeStruct((NB, M, KVL), BF16),
        compiler_params=_cparams("arbitrary", "arbitrary"),
        name="attn_sample",
    )(page_table, q, knew, *([cache_latent] * PG), *([cache_krope] * PG))


def _s5_kernel(dims, u_ref, h0r_ref, h0i_ref, ar_ref, ai_ref, bre_ref, bim_ref, cre_ref, cim_ref,
               d_ref, wglu_ref, bglu_ref, o_ref, fr_ref, fi_ref, sr_ref, si_ref, hr_ref, hi_ref):
    NB, tt, SW, NS, CW, RG = dims
    i = pl.program_id(0)

    @pl.when(i == 0)
    def _():
        sr_ref[...] = h0r_ref[...]
        si_ref[...] = h0i_ref[...]

    u = u_ref[...]
    ub = u.astype(BF16)
    nb_cols = bre_ref.shape[2]
    for n in range(NS // nb_cols):
        k0 = (n * nb_cols // (NS // SW)) // LANES * LANES
        uu = ub[:, k0:k0 + LANES]
        hr_ref[:, n * nb_cols:(n + 1) * nb_cols] = _dot(uu, bre_ref[n])
        hi_ref[:, n * nb_cols:(n + 1) * nb_cols] = _dot(uu, bim_ref[n])

    for cc in range(NS // CW):
        cols = slice(cc * CW, (cc + 1) * CW)
        ar = ar_ref[:, cols]
        ai = ai_ref[:, cols]
        for rg in range(NB // RG):
            rows = slice(rg * RG, (rg + 1) * RG)

            def body(t, carry):
                h_re, h_im = carry
                r0 = pl.multiple_of(t * NB + rg * RG, SUBLANES)
                n_re = ar * h_re - ai * h_im + hr_ref[pl.ds(r0, RG), cols]
                n_im = ar * h_im + ai * h_re + hi_ref[pl.ds(r0, RG), cols]
                hr_ref[pl.ds(r0, RG), cols] = n_re
                hi_ref[pl.ds(r0, RG), cols] = n_im
                return n_re, n_im

            h_re, h_im = lax.fori_loop(0, tt, body, (sr_ref[rows, cols], si_ref[rows, cols]),
                                       unroll=min(tt, 4))
            sr_ref[rows, cols] = h_re
            si_ref[rows, cols] = h_im

    kc = cre_ref.shape[1]
    ys = []
    for j in range(SW // cre_ref.shape[2]):
        hb_re = hr_ref[:, j * kc:(j + 1) * kc].astype(BF16)
        hb_im = hi_ref[:, j * kc:(j + 1) * kc].astype(BF16)
        ys.append(_dot(hb_re, cre_ref[j]) - _dot(hb_im, cim_ref[j]))
    y = jnp.concatenate(ys, axis=-1) + d_ref[...] * u
    g = jax.nn.gelu(y)
    o_ref[...] = (g * jax.nn.sigmoid(_dot(g.astype(BF16), wglu_ref[...]) + bglu_ref[...])).astype(o_ref.dtype)

    @pl.when(i == pl.num_programs(0) - 1)
    def _():
        fr_ref[...] = sr_ref[...]
        fi_ref[...] = si_ref[...]


def _s5(u_tm, h0_re, h0_im, p, NB, tt):
    R, SW = u_tm.shape
    NS = h0_re.shape[1]
    rows = tt * NB
    CW = 512
    RG = 16
    assert NB % RG == 0 and NS % CW == 0 and R % rows == 0
    full = lambda a: pl.BlockSpec(a.shape, lambda i, n=a.ndim: (0,) * n)
    ws = [p["abar_re"], p["abar_im"], p["b_re"], p["b_im"], p["c_re"], p["c_im"], p["ssm_d"], p["w_glu"], p["b_glu"]]
    return pl.pallas_call(
        functools.partial(_s5_kernel, (NB, tt, SW, NS, CW, RG)),
        grid=(R // rows,),
        in_specs=[pl.BlockSpec((rows, SW), lambda i: (i, 0)), full(h0_re), full(h0_im)] + [full(w) for w in ws],
        out_specs=[pl.BlockSpec((rows, SW), lambda i: (i, 0)),
                   pl.BlockSpec((NB, NS), lambda i: (0, 0)), pl.BlockSpec((NB, NS), lambda i: (0, 0))],
        out_shape=[jax.ShapeDtypeStruct((R, SW), BF16),
                   jax.ShapeDtypeStruct((NB, NS), F32), jax.ShapeDtypeStruct((NB, NS), F32)],
        scratch_shapes=[pltpu.VMEM((NB, NS), F32), pltpu.VMEM((NB, NS), F32),
                        pltpu.VMEM((rows, NS), F32), pltpu.VMEM((rows, NS), F32)],
        compiler_params=_cparams("arbitrary"),
        name="s5_scan",
    )(u_tm, h0_re, h0_im, *ws)


def _merge_kernel(dims, x_ref, ol_ref, os_ref, ga_ref, gs_ref, gta_ref, shm_ref, scm_ref,
                  wuv_ref, wap_ref, wsp_ref, wout_ref, gffn_ref, wrt_ref,
                  xo_ref, h2_ref, lg_ref):
    n_slab, slab_k = dims
    ol = ol_ref[...]
    o = jnp.concatenate([_dot(ol[:, p * slab_k:(p + 1) * slab_k], wuv_ref[p]) for p in range(n_slab)], axis=-1)
    a = _dot(o.astype(BF16), wap_ref[...])
    s = _dot(os_ref[...], wsp_ref[...])
    merged = (ga_ref[...] * a + gs_ref[...] * s).astype(BF16)
    m = _dot(merged, wout_ref[...])
    x = x_ref[...] + gta_ref[...] * m
    xo_ref[...] = x
    h2 = _rms(x, gffn_ref[...]) * (1.0 + scm_ref[...]) + shm_ref[...]
    h2_ref[...] = h2
    lg_ref[...] = _dot_nt(wrt_ref[...], h2.astype(BF16))


def _merge(x, o_lat, o_s, ga, gs, mod, per_row_mod, p, tm):
    NB, TT, D = x.shape
    HK = o_lat.shape[-1]
    SW = o_s.shape[-1]
    E = p["w_router_t"].shape[0]
    n_slab = p["w_uvp"].shape[0]
    slab_k = p["w_uvp"].shape[1]
    if per_row_mod:
        mod_spec = lambda k: pl.BlockSpec((None, tm, D), lambda b, i, k=k: (b, i, k))
    else:
        mod_spec = lambda k: pl.BlockSpec((None, 1, D), lambda b, i, k=k: (b, 0, k))
    row = lambda w: pl.BlockSpec((None, tm, w), lambda b, i: (b, i, 0))
    full = lambda a: pl.BlockSpec(a.shape, lambda b, i, n=a.ndim: (0,) * n)
    ws = [p["w_uvp"], p["w_attn_proj"], p["w_ssm_proj"], p["w_out"], p["g_ffn"], p["w_router_t"]]
    return pl.pallas_call(
        functools.partial(_merge_kernel, (n_slab, slab_k)),
        grid=(NB, TT // tm),
        in_specs=[row(D), row(HK), row(SW), row(D), row(D), mod_spec(2), mod_spec(3), mod_spec(4)]
                 + [full(w) for w in ws],
        out_specs=[row(D), row(D), pl.BlockSpec((None, E, tm), lambda b, i: (b, 0, i))],
        out_shape=[jax.ShapeDtypeStruct((NB, TT, D), F32), jax.ShapeDtypeStruct((NB, TT, D), F32),
                   jax.ShapeDtypeStruct((NB, E, TT), F32)],
        compiler_params=_cparams("arbitrary", "arbitrary"),
        name="merge_out_proj",
    )(x, o_lat, o_s, ga, gs, mod, mod, mod, *ws)


def _first_index(mask, iota, big):
    return jnp.min(jnp.where(mask, iota, big), axis=0, keepdims=True)


def _route_kernel(dims, lg_ref, bias_ref, tri_ref, idx_ref, wts_ref, rank_ref, cnt_ref, base_ref):
    E, NG = dims
    GS = E // NG
    first = (pl.program_id(0) == 0) & (pl.program_id(1) == 0)

    @pl.when(first)
    def _():
        base_ref[...] = jnp.zeros(base_ref.shape, F32)

    scores = jax.nn.sigmoid(lg_ref[...])
    choice = scores + bias_ref[...]
    tm = scores.shape[1]
    eio = lax.broadcasted_iota(jnp.int32, (E, tm), 0).astype(F32)
    gio = lax.broadcasted_iota(jnp.int32, (GS, tm), 0).astype(F32)
    gscore = []
    for g in range(NG):
        cg = choice[g * GS:(g + 1) * GS]
        m1 = jnp.max(cg, axis=0, keepdims=True)
        i1 = _first_index(cg == m1, gio, float(GS))
        m2 = jnp.max(jnp.where(gio == i1, -jnp.inf, cg), axis=0, keepdims=True)
        gscore.append(m1 + m2)
    gscore = jnp.concatenate(gscore, axis=0)
    nio = lax.broadcasted_iota(jnp.int32, (NG, tm), 0).astype(F32)
    gsel = jnp.zeros((NG, tm), F32)
    rem = gscore
    for _ in range(TOPK_GROUPS):
        mg = jnp.max(rem, axis=0, keepdims=True)
        ig = _first_index(rem == mg, nio, float(NG))
        hit = nio == ig
        gsel = jnp.where(hit, 1.0, gsel)
        rem = jnp.where(hit, -jnp.inf, rem)
    allowed = jnp.concatenate(
        [jnp.broadcast_to(gsel[g:g + 1], (GS, tm)) for g in range(NG)], axis=0) > 0.5
    rem = jnp.where(allowed, choice, -jnp.inf)
    picks, vals = [], []
    chosen = jnp.zeros((E, tm), F32)
    for _ in range(TOP_K):
        mk = jnp.max(rem, axis=0, keepdims=True)
        ik = _first_index(rem == mk, eio, float(E))
        hit = eio == ik
        picks.append(ik)
        vals.append(jnp.sum(jnp.where(hit, scores, 0.0), axis=0, keepdims=True))
        chosen = chosen + hit.astype(F32)
        rem = jnp.where(hit, -jnp.inf, rem)
    total = vals[0]
    for v in vals[1:]:
        total = total + v
    prefix = _dot(chosen.astype(BF16), tri_ref[...]) + base_ref[...]
    ranks = [jnp.sum(jnp.where(eio == ik, prefix, 0.0), axis=0, keepdims=True) for ik in picks]
    base_ref[...] = base_ref[...] + jnp.sum(chosen, axis=1, keepdims=True)
    pad_i = [jnp.zeros((1, tm), jnp.int32)] * (TOP_K_PAD - TOP_K)
    pad_f = [jnp.zeros((1, tm), F32)] * (TOP_K_PAD - TOP_K)
    idx_ref[...] = jnp.concatenate([ik.astype(jnp.int32) for ik in picks] + pad_i, axis=0)
    wts_ref[...] = jnp.concatenate([v / total * ROUTED_SCALE for v in vals] + pad_f, axis=0)
    rank_ref[...] = jnp.concatenate([r.astype(jnp.int32) for r in ranks] + pad_i, axis=0)
    cnt_ref[...] = jnp.broadcast_to(base_ref[...], cnt_ref.shape)


def _route(logits_t, bias, tri, tm):
    NB, E, TT = logits_t.shape
    blk = lambda: pl.BlockSpec((None, TOP_K_PAD, tm), lambda b, i: (b, 0, i))
    return pl.pallas_call(
        functools.partial(_route_kernel, (E, N_EXPERT_GROUPS)),
        grid=(NB, TT // tm),
        in_specs=[pl.BlockSpec((None, E, tm), lambda b, i: (b, 0, i)),
                  pl.BlockSpec((E, 1), lambda b, i: (0, 0)),
                  pl.BlockSpec((tm, tm), lambda b, i: (0, 0))],
        out_specs=[blk(), blk(), blk(), pl.BlockSpec((E, LANES), lambda b, i: (0, 0))],
        out_shape=[jax.ShapeDtypeStruct((NB, TOP_K_PAD, TT), jnp.int32),
                   jax.ShapeDtypeStruct((NB, TOP_K_PAD, TT), F32),
                   jax.ShapeDtypeStruct((NB, TOP_K_PAD, TT), jnp.int32),
                   jax.ShapeDtypeStruct((E, LANES), F32)],
        scratch_shapes=[pltpu.VMEM((E, 1), F32)],
        compiler_params=_cparams("arbitrary", "arbitrary"),
        name="route_topk",
    )(logits_t, bias, tri)


def _gather_rows(idx_hbm, src_hbm, idx_smem, buf, isem, rsem, step, n_steps, rows):
    slot = step % 2
    nslot = 1 - slot

    def idx_copy(s, sl):
        return pltpu.make_async_copy(idx_hbm.at[s], idx_smem.at[sl], isem.at[sl])

    def issue(sl):
        for r in range(rows):
            pltpu.make_async_copy(src_hbm.at[idx_smem[sl, r]], buf.at[sl, r], rsem.at[sl]).start()

    @pl.when(step == 0)
    def _():
        c = idx_copy(0, 0)
        c.start()
        c.wait()
        issue(0)

        @pl.when(n_steps > 1)
        def _():
            idx_copy(1, 1).start()

    @pl.when(step + 1 < n_steps)
    def _():
        idx_copy(step + 1, nslot).wait()
        issue(nslot)

    pltpu.make_async_copy(src_hbm.at[pl.ds(0, rows)], buf.at[slot], rsem.at[slot]).wait()

    @pl.when(step + 2 < n_steps)
    def _():
        idx_copy(step + 2, slot).start()

    return slot


def _experts_kernel(be_ref, tok_hbm, x_hbm, w1_ref, w3_ref, w2_ref, y_ref, idx_smem, xbuf, isem, rsem):
    step = pl.program_id(0)
    rows = xbuf.shape[1]
    slot = _gather_rows(tok_hbm, x_hbm, idx_smem, xbuf, isem, rsem, step, pl.num_programs(0), rows)
    xb = xbuf[slot].astype(BF16)
    h1 = _dot(xb, w1_ref[...])
    h3 = _dot(xb, w3_ref[...])
    a = (h1 * jax.nn.sigmoid(h1) * h3).astype(BF16)
    y_ref[...] = _dot(a, w2_ref[...])


def _experts(block_e, buf_tok, h2, w1, w3, w2):
    n_blocks, rows = buf_tok.shape
    N, D = h2.shape
    _, _, F = w1.shape
    grid_spec = pltpu.PrefetchScalarGridSpec(
        num_scalar_prefetch=1, grid=(n_blocks,),
        in_specs=[pl.BlockSpec(memory_space=pl.ANY), pl.BlockSpec(memory_space=pl.ANY),
                  pl.BlockSpec((None, D, F), lambda i, be: (be[i], 0, 0)),
                  pl.BlockSpec((None, D, F), lambda i, be: (be[i], 0, 0)),
                  pl.BlockSpec((None, F, D), lambda i, be: (be[i], 0, 0))],
        out_specs=pl.BlockSpec((rows, D), lambda i, be: (i, 0)),
        scratch_shapes=[pltpu.SMEM((2, rows), jnp.int32), pltpu.VMEM((2, rows, D), F32),
                        pltpu.SemaphoreType.DMA((2,)), pltpu.SemaphoreType.DMA((2,))],
    )
    return pl.pallas_call(
        _experts_kernel, grid_spec=grid_spec,
        out_shape=jax.ShapeDtypeStruct((n_blocks * rows, D), F32),
        compiler_params=_cparams("arbitrary"),
        name="routed_experts",
    )(block_e, buf_tok, h2, w1, w3, w2)


def _combine_kernel(dims, pos_hbm, y_hbm, x_ref, h2_ref, wts_ref, gtm_ref, w1_ref, w3_ref, w2_ref, gfin_ref,
                    xo_ref, yo_ref, idx_smem, ybuf, isem, rsem):
    tm, final = dims
    step = pl.program_id(0)
    slot = _gather_rows(pos_hbm, y_hbm, idx_smem, ybuf, isem, rsem, step, pl.num_programs(0), TOP_K * tm)
    wts = wts_ref[...]
    routed = wts[:, 0:1] * ybuf[slot, 0:tm]
    for k in range(1, TOP_K):
        routed = routed + wts[:, k:k + 1] * ybuf[slot, k * tm:(k + 1) * tm]
    hb = h2_ref[...].astype(BF16)
    h1 = _dot(hb, w1_ref[...])
    h3 = _dot(hb, w3_ref[...])
    shared = _dot((h1 * jax.nn.sigmoid(h1) * h3).astype(BF16), w2_ref[...])
    x = x_ref[...] + gtm_ref[...] * (routed + shared)
    xo_ref[...] = x
    if final:
        yo_ref[...] = _rms(x, gfin_ref[...])
    else:
        yo_ref[...] = x


def _combine(pos, y_sorted, x, h2, wts, gtm, per_row_mod, p, g_final, final, tm, tokens_per_mod):
    N, D = x.shape
    F = p["w_s1"].shape[1]
    n_steps = N // tm
    if per_row_mod:
        gt_spec = pl.BlockSpec((None, tm, D), lambda i: (0, i, 5))
    else:
        per = tokens_per_mod // tm
        gt_spec = pl.BlockSpec((None, 1, D), lambda i: (i // per, 0, 5))
    full = lambda a: pl.BlockSpec(a.shape, lambda i, n=a.ndim: (0,) * n)
    row = lambda w: pl.BlockSpec((tm, w), lambda i: (i, 0))
    return pl.pallas_call(
        functools.partial(_combine_kernel, (tm, final)),
        grid=(n_steps,),
        in_specs=[pl.BlockSpec(memory_space=pl.ANY), pl.BlockSpec(memory_space=pl.ANY),
                  row(D), row(D), row(TOP_K_PAD), gt_spec,
                  full(p["w_s1"]), full(p["w_s3"]), full(p["w_s2"]), full(g_final)],
        out_specs=[row(D), row(D)],
        out_shape=[jax.ShapeDtypeStruct((N, D), F32), jax.ShapeDtypeStruct((N, D), F32)],
        scratch_shapes=[pltpu.SMEM((2, TOP_K * tm), jnp.int32), pltpu.VMEM((2, TOP_K * tm, D), F32),
                        pltpu.SemaphoreType.DMA((2,)), pltpu.SemaphoreType.DMA((2,))],
        compiler_params=_cparams("arbitrary"),
        name="combine_shared",
    )(pos, y_sorted, x, h2, wts, gtm, p["w_s1"], p["w_s3"], p["w_s2"], g_final)


def _prep_layer(l, w, cfg):
    H, QL, KVL, ROPE, NOPE, SW, D, G, P, C = cfg
    half = ROPE // 2
    w_in = w["w_in"][l]
    o_kpe = QL + KVL
    o_u = o_kpe + ROPE
    o_ga = o_u + SW
    k1 = w_in[:, o_kpe:o_kpe + half]
    k2 = w_in[:, o_kpe + half:o_kpe + ROPE]
    zpad = jnp.zeros((D, LANES - ROPE), F32)
    w_a = jnp.concatenate([w_in[:, :o_kpe], k1, k2, zpad, k2, k1, zpad, w_in[:, o_u:o_ga]], axis=1)
    w_uq = w["w_uq"][l]
    w_q = jnp.concatenate([w_uq[:, :, :NOPE].reshape(QL, H * NOPE),
                           w_uq[:, :, NOPE:NOPE + half].reshape(QL, H * half),
                           w_uq[:, :, NOPE + half:].reshape(QL, H * half)], axis=1)
    hps = LANES // NOPE
    ukt = jnp.swapaxes(w["w_uk"][l], 1, 2)
    w_ukp = jnp.zeros((H // hps, LANES, hps * KVL), F32)
    for hh in range(H):
        pp, j = divmod(hh, hps)
        w_ukp = w_ukp.at[pp, j * NOPE:(j + 1) * NOPE, j * KVL:(j + 1) * KVL].set(ukt[hh])
    uv = w["w_uv"][l]
    V = uv.shape[-1]
    hpv = LANES // V
    w_uvp = jnp.zeros((H // hpv, hpv * KVL, LANES), F32)
    for hh in range(H):
        pp, j = divmod(hh, hpv)
        w_uvp = w_uvp.at[pp, j * KVL:(j + 1) * KVL, j * V:(j + 1) * V].set(uv[hh])
    a_re, a_im = w["ssm_a_re"][l], w["ssm_a_im"][l]
    dt = jnp.exp(w["ssm_log_dt"][l])[:, None]
    mag = jnp.exp(dt * a_re)
    abar_re, abar_im = mag * jnp.cos(dt * a_im), mag * jnp.sin(dt * a_im)
    den = a_re * a_re + a_im * a_im
    coef_re = ((abar_re - 1.0) * a_re + abar_im * a_im) / den
    coef_im = (abar_im * a_re - (abar_re - 1.0) * a_im) / den
    b_re, b_im = w["ssm_b_re"][l], w["ssm_b_im"][l]
    bbar_re = coef_re[..., None] * b_re - coef_im[..., None] * b_im
    bbar_im = coef_re[..., None] * b_im + coef_im[..., None] * b_re
    NS = G * P
    NBC = 2 * LANES
    gpb = NBC // P

    def in_blocks(bb):
        blk = jnp.zeros((NS // NBC, LANES, NBC), F32)
        for g in range(G):
            n, j = divmod(g, gpb)
            r0 = (g * C) % LANES
            blk = blk.at[n, r0:r0 + C, j * P:(j + 1) * P].set(bb[g].T)
        return blk.astype(BF16)

    OC = 2 * LANES
    gpo = OC // C

    def out_blocks(cc):
        blk = jnp.zeros((SW // OC, gpo * P, OC), F32)
        for g in range(G):
            n, j = divmod(g, gpo)
            blk = blk.at[n, j * P:(j + 1) * P, j * C:(j + 1) * C].set(cc[g].T)
        return blk.astype(BF16)

    return dict(
        g_mix=w["g_mix"][l][None], w_a=w_a.astype(BF16), w_g=w_in[:, o_ga:].astype(BF16),
        g_qnorm=w["g_qnorm"][l][None], w_q=w_q.astype(BF16), w_ukp=w_ukp.astype(BF16),
        g_kvnorm=w["g_kvnorm"][l][None], w_uvp=w_uvp.astype(BF16),
        abar_re=abar_re.reshape(1, NS), abar_im=abar_im.reshape(1, NS),
        b_re=in_blocks(bbar_re), b_im=in_blocks(bbar_im),
        c_re=out_blocks(w["ssm_c_re"][l]), c_im=out_blocks(w["ssm_c_im"][l]),
        ssm_d=w["ssm_d"][l].reshape(1, SW), w_glu=w["w_glu"][l].astype(BF16), b_glu=w["b_glu"][l][None],
        w_attn_proj=w["w_attn_proj"][l].astype(BF16), w_ssm_proj=w["w_ssm_proj"][l].astype(BF16),
        w_out=w["w_out"][l].astype(BF16), g_ffn=w["g_ffn"][l][None],
        w_router_t=w["w_router"][l].T.astype(BF16), router_bias=w["router_bias"][l][:, None],
        w_e1=w["w_e1"][l].astype(BF16), w_e3=w["w_e3"][l].astype(BF16), w_e2=w["w_e2"][l].astype(BF16),
        w_s1=w["w_s1"][l].astype(BF16), w_s3=w["w_s3"][l].astype(BF16), w_s2=w["w_s2"][l].astype(BF16),
    )


def _rope_tabs(pos, H, ROPE):
    half = ROPE // 2
    inv = ROPE_THETA ** (-jnp.arange(0, ROPE, 2, dtype=F32) / ROPE)
    ang = pos.astype(F32)[:, None] * inv[None, :]
    cos, sin = jnp.cos(ang), jnp.sin(ang)
    T = pos.shape[0]
    zpad = jnp.zeros((T, LANES - ROPE), F32)
    return (jnp.tile(cos, (1, H)), jnp.tile(sin, (1, H)),
            jnp.concatenate([cos, cos, zpad], axis=1), jnp.concatenate([-sin, sin, zpad], axis=1))


def _head_select(H, ROPE):
    half = ROPE // 2
    sel = np.zeros((2 * H * half, H * LANES), np.float32)
    for hh in range(H):
        for i in range(half):
            sel[hh * half + i, hh * LANES + i] = 1.0
            sel[H * half + hh * half + i, hh * LANES + half + i] = 1.0
    return jnp.asarray(sel, BF16)


def _moe(x2, h2, logits_t, gtm, per_row_mod, tokens_per_mod, p, tri, g_final, final, tm_route, tm_comb):
    N, D = x2.shape
    NB, E, TT = logits_t.shape
    idx_t, wts_t, rank_t, cnt = _route(logits_t, p["router_bias"], tri, tm_route)
    to_rows = lambda a: jnp.swapaxes(a, 1, 2).reshape(N, TOP_K_PAD)
    idx, wts, rank = to_rows(idx_t), to_rows(wts_t), to_rows(rank_t)
    counts = cnt[:, 0].astype(jnp.int32)
    padded = (counts + MOE_ROWS - 1) // MOE_ROWS * MOE_ROWS
    pad_end = jnp.cumsum(padded)
    pad_start = pad_end - padded
    A = N * TOP_K
    n_blocks = -(-A // MOE_ROWS) + E
    dest = pad_start[idx[:, :TOP_K]] + rank[:, :TOP_K]
    tok = jnp.broadcast_to(jnp.arange(N, dtype=jnp.int32)[:, None], (N, TOP_K))
    buf_tok = jnp.zeros((n_blocks * MOE_ROWS,), jnp.int32).at[dest.reshape(A)].set(tok.reshape(A))
    block_e = jnp.minimum(jnp.searchsorted(pad_end, jnp.arange(n_blocks, dtype=jnp.int32) * MOE_ROWS, side="right"),
                          E - 1).astype(jnp.int32)
    y_sorted = _experts(block_e, buf_tok.reshape(n_blocks, MOE_ROWS), h2, p["w_e1"], p["w_e3"], p["w_e2"])
    pos = jnp.swapaxes(dest.reshape(N // tm_comb, tm_comb, TOP_K), 1, 2).reshape(N // tm_comb, TOP_K * tm_comb)
    return _combine(pos, y_sorted, x2, h2, wts, gtm, per_row_mod, p, g_final, final, tm_comb, tokens_per_mod)


def kernel(x_prompt, x_sample, cache_latent, cache_krope, state_ssm_re, state_ssm_im, page_table, c_prompt, c_sample, w_ada, b_ada, g_mix, w_in, g_qnorm, w_uq, g_kvnorm, w_uk, w_uv, ssm_a_re, ssm_a_im, ssm_log_dt, ssm_b_re, ssm_b_im, ssm_c_re, ssm_c_im, ssm_d, w_glu, b_glu, w_attn_proj, w_ssm_proj, w_out, g_ffn, w_router, router_bias, w_e1, w_e3, w_e2, w_s1, w_s3, w_s2, g_final):
    w = dict(w_in=w_in, g_mix=g_mix, g_qnorm=g_qnorm, w_uq=w_uq, g_kvnorm=g_kvnorm, w_uk=w_uk, w_uv=w_uv,
             ssm_a_re=ssm_a_re, ssm_a_im=ssm_a_im, ssm_log_dt=ssm_log_dt, ssm_b_re=ssm_b_re, ssm_b_im=ssm_b_im,
             ssm_c_re=ssm_c_re, ssm_c_im=ssm_c_im, ssm_d=ssm_d, w_glu=w_glu, b_glu=b_glu,
             w_attn_proj=w_attn_proj, w_ssm_proj=w_ssm_proj, w_out=w_out, g_ffn=g_ffn, w_router=w_router,
             router_bias=router_bias, w_e1=w_e1, w_e3=w_e3, w_e2=w_e2, w_s1=w_s1, w_s3=w_s3, w_s2=w_s2)
    BP, TP, D = x_prompt.shape
    BS, TS, _ = x_sample.shape
    L = w_in.shape[0]
    QL, H, QKD = w_uq.shape[1:]
    KVL = cache_latent.shape[-1]
    ROPE = cache_krope.shape[-1]
    PAGE = cache_latent.shape[2]
    NOPE = QKD - ROPE
    G, P = ssm_a_re.shape[1:]
    C = SSM_GROUP
    SW = G * C
    NS = G * P
    E = w_router.shape[-1]
    past_len = page_table.shape[1] * PAGE
    scale = float(QKD) ** -0.5
    cfg = (H, QL, KVL, ROPE, NOPE, SW, D, G, P, C)
    dims = (H, QL, KVL, ROPE, NOPE, SW, D)
    assert H * ROPE // 2 == LANES and LANES % NOPE == 0 and KVL % LANES == 0

    NSAMP = BS * TS
    tm_p = min(256, TP)
    tm_s = min(256, NSAMP)
    tq = min(256, TP)
    tt_p = max(1, min(TP, 512 // BP))

    mod_all = _modulation(jnp.concatenate([c_prompt, c_sample], axis=0), w_ada.astype(BF16), b_ada[:, None, :])
    tabs_p = _rope_tabs(jnp.arange(TP, dtype=jnp.int32), H, ROPE)
    tabs_s1 = _rope_tabs(past_len + jnp.arange(TS, dtype=jnp.int32), H, ROPE)
    tabs_s = tuple(jnp.tile(t, (BS, 1)) for t in tabs_s1)
    sel = _head_select(H, ROPE)
    tri_p = jnp.asarray(np.triu(np.ones((tm_p, tm_p), np.float32), 1), BF16)
    tri_s = jnp.asarray(np.triu(np.ones((tm_s, tm_s), np.float32), 1), BF16)
    g_fin = g_final[None]
    zeros_state = jnp.zeros((BP, NS), F32)

    xp = x_prompt
    xs = x_sample.reshape(1, NSAMP, D)
    outs = {k: [] for k in ("lat_p", "kpe_p", "sre_p", "sim_p", "lat_s", "kpe_s", "sre_s", "sim_s")}
    yp = ys = None
    for l in range(L):
        p = _prep_layer(l, w, cfg)
        p["sel"] = sel
        final = l == L - 1
        mod_p = mod_all[l, :BP][:, None, :]
        mod_s = jnp.repeat(mod_all[l, BP:], TS, axis=0)[None]

        q, kcat, kv, kpe, u, ga, gs = _in_proj(xp, mod_p, False, p, tabs_p, dims, tm_p)
        outs["lat_p"].append(kv)
        outs["kpe_p"].append(kpe)
        o_lat = _attn_prompt(q, kcat, KVL, scale, tq)
        u_tm = jnp.swapaxes(u, 0, 1).reshape(TP * BP, SW)
        o_s_tm, f_re, f_im = _s5(u_tm, zeros_state, zeros_state, p, BP, tt_p)
        outs["sre_p"].append(f_re.reshape(BP, G, P))
        outs["sim_p"].append(f_im.reshape(BP, G, P))
        o_s = jnp.swapaxes(o_s_tm.reshape(TP, BP, SW), 0, 1)
        x1, h2, lg = _merge(xp, o_lat, o_s, ga, gs, mod_p, False, p, tm_p)
        xo, yo = _moe(x1.reshape(BP * TP, D), h2.reshape(BP * TP, D), lg, mod_p, False, TP, p, tri_p,
                      g_fin, final, tm_p, min(128, TP))
        xp = xo.reshape(BP, TP, D)
        yp = yo.reshape(BP, TP, D)

        q, kcat, kv, kpe, u, ga, gs = _in_proj(xs, mod_s, True, p, tabs_s, dims, tm_s)
        outs["lat_s"].append(kv.reshape(BS, TS, KVL))
        outs["kpe_s"].append(kpe.reshape(BS, TS, ROPE))
        KC = KVL + LANES
        q_s = jnp.swapaxes(q.reshape(H, BS, TS, KC), 0, 1).reshape(BS, H * TS, KC)
        knew = jnp.pad(kcat.reshape(BS, TS, KC), ((0, 0), (0, LANES - TS), (0, 0)))
        o_lat_s = _attn_sample(q_s, knew, cache_latent, cache_krope, page_table, l, scale, H, TS)
        o_lat_s = jnp.swapaxes(o_lat_s.reshape(BS, H, TS, KVL), 1, 2).reshape(1, NSAMP, H * KVL)
        u_tm = jnp.swapaxes(u.reshape(BS, TS, SW), 0, 1).reshape(TS * BS, SW)
        o_s_tm, f_re, f_im = _s5(u_tm, state_ssm_re[l].reshape(BS, NS), state_ssm_im[l].reshape(BS, NS), p, BS, TS)
        outs["sre_s"].append(f_re.reshape(BS, G, P))
        outs["sim_s"].append(f_im.reshape(BS, G, P))
        o_s = jnp.swapaxes(o_s_tm.reshape(TS, BS, SW), 0, 1).reshape(1, NSAMP, SW)
        x1, h2, lg = _merge(xs, o_lat_s, o_s, ga, gs, mod_s, True, p, tm_s)
        xo, yo = _moe(x1.reshape(NSAMP, D), h2.reshape(NSAMP, D), lg, mod_s, True, NSAMP, p, tri_s,
                      g_fin, final, tm_s, min(128, NSAMP))
        xs = xo.reshape(1, NSAMP, D)
        ys = yo.reshape(BS, TS, D)

    st = lambda k: jnp.stack(outs[k])
    return (yp, ys, st("lat_p"), st("kpe_p"), st("sre_p"), st("sim_p"),
            st("lat_s"), st("kpe_s"), st("sre_s"), st("sim_s"))
```

```python
import functools
import math

import jax
import jax.numpy as jnp
import numpy as np
from jax import lax
from jax.experimental import pallas as pl
from jax.experimental.pallas import tpu as pltpu

F32 = jnp.float32
BF16 = jnp.bfloat16

ROPE_THETA = 10000.0
RMS_EPS = 1e-6
N_EXPERT_GROUPS = 8
TOPK_GROUPS = 4
TOP_K = 6
ROUTED_SCALE = 2.5
SSM_GROUP = 16

LANES = 128
SUBLANES = 8
VMEM_LIMIT_BYTES = 56 * 1024 * 1024

NEG_BIG = -1e30
TOP_K_PAD = 8
MOE_TOKENS = 128
PAGES_PER_STEP = 64
PAGES_PER_DOT = 16
NEW_KEY_ROWS = 16


def _cparams(*sem):
    return pltpu.CompilerParams(dimension_semantics=sem, vmem_limit_bytes=VMEM_LIMIT_BYTES)


def _dot(a, b):
    return jnp.dot(a, b, preferred_element_type=F32)


def _dot_nt(a, b):
    return lax.dot_general(a, b, (((1,), (1,)), ((), ())), preferred_element_type=F32)


def _rms(x, g):
    return x * lax.rsqrt(jnp.mean(x * x, axis=-1, keepdims=True) + RMS_EPS) * g


def _pack_bf16_pairs(x):
    bits = lax.bitcast_convert_type(x.astype(F32), jnp.uint32)
    n = x.shape[1] // 2
    return bits[:, :n] | (bits[:, n:] >> 16)


def _unpack_bf16_pairs(w):
    hi = lax.bitcast_convert_type(w & jnp.uint32(0xFFFF0000), F32).astype(BF16)
    lo = lax.bitcast_convert_type(w << 16, F32).astype(BF16)
    return hi, lo


def _swiglu_packed(xw, w1_ref, w3_ref, w2_ref):
    xa, xb = _unpack_bf16_pairs(xw)
    n = xw.shape[1]
    h1 = _dot(xa, w1_ref[:n, :]) + _dot(xb, w1_ref[n:, :])
    h3 = _dot(xa, w3_ref[:n, :]) + _dot(xb, w3_ref[n:, :])
    return _dot((h1 * jax.nn.sigmoid(h1) * h3).astype(BF16), w2_ref[...])


def _mod_kernel(c_ref, w_ref, b_ref, o_ref):
    c = c_ref[...]
    a = (c * jax.nn.sigmoid(c)).astype(BF16)
    o_ref[...] = _dot(a, w_ref[...]) + b_ref[...]


def _modulation(c_all, w_ada, b_ada):
    L, D, D6 = w_ada.shape
    R = c_all.shape[0]
    return pl.pallas_call(
        _mod_kernel,
        grid=(L, D6 // D),
        in_specs=[pl.BlockSpec((R, D), lambda l, j: (0, 0)),
                  pl.BlockSpec((None, D, D), lambda l, j: (l, 0, j)),
                  pl.BlockSpec((None, 1, D), lambda l, j: (l, 0, j))],
        out_specs=pl.BlockSpec((None, R, D), lambda l, j: (l, 0, j)),
        out_shape=jax.ShapeDtypeStruct((L, R, D6), F32),
        compiler_params=_cparams("arbitrary", "arbitrary"),
        name="adaln_mod",
    )(c_all, w_ada, b_ada)


def _in_proj_kernel(dims, x_ref, sh_ref, sc_ref, gmix_ref, wa_ref, wg_ref, gq_ref, wq_ref, wuk_ref,
                    sel_ref, gkv_ref, c8_ref, s8_ref, ct_ref, st_ref,
                    q_ref, kcat_ref, kv_ref, kpe_ref, u_ref, ga_ref, gs_ref):
    H, QL, KVL, ROPE, NOPE, SW, D = dims
    half = H * ROPE // 2
    x = x_ref[...]
    h = _rms(x, gmix_ref[...]) * (1.0 + sc_ref[...]) + sh_ref[...]
    hb = h.astype(BF16)
    za = _dot(hb, wa_ref[...])
    zg = _dot(hb, wg_ref[...])
    ga_ref[...] = jax.nn.sigmoid(zg[:, :D])
    gs_ref[...] = jax.nn.sigmoid(zg[:, D:])
    o0 = QL
    o1 = o0 + KVL
    o2 = o1 + LANES
    o3 = o2 + LANES
    u_ref[...] = za[:, o3:o3 + SW]
    kv = _rms(za[:, o0:o1], gkv_ref[...])
    kv_ref[...] = kv
    krot = za[:, o1:o2] * ct_ref[...] + za[:, o2:o3] * st_ref[...]
    kpe_ref[...] = krot[:, :ROPE]
    kcat_ref[:, :KVL] = kv.astype(BF16)
    kcat_ref[:, KVL:] = krot.astype(BF16)
    cqn = _rms(za[:, :QL], gq_ref[...]).astype(BF16)
    qq = _dot(cqn, wq_ref[...])
    qn = qq[:, :H * NOPE]
    x1 = qq[:, H * NOPE:H * NOPE + half]
    x2 = qq[:, H * NOPE + half:]
    c8 = c8_ref[...]
    s8 = s8_ref[...]
    rot = jnp.concatenate([x1 * c8 - x2 * s8, x2 * c8 + x1 * s8], axis=-1).astype(BF16)
    qpe = _dot(rot, sel_ref[...])
    heads_per_slab = LANES // NOPE
    for p in range(H // heads_per_slab):
        ql = _dot(qn[:, p * LANES:(p + 1) * LANES].astype(BF16), wuk_ref[p])
        for j in range(heads_per_slab):
            hh = p * heads_per_slab + j
            q_ref[hh, :, :KVL] = ql[:, j * KVL:(j + 1) * KVL].astype(BF16)
    for hh in range(H):
        q_ref[hh, :, KVL:] = qpe[:, hh * LANES:(hh + 1) * LANES].astype(BF16)


def _in_proj(x, mod, per_row_mod, p, tabs, dims, tm):
    H, QL, KVL, ROPE, NOPE, SW, D = dims
    NB, TT, _ = x.shape
    KC = KVL + LANES
    grid = (NB, TT // tm)
    if per_row_mod:
        mod_spec = lambda k: pl.BlockSpec((None, tm, D), lambda b, i, k=k: (b, i, k))
    else:
        mod_spec = lambda k: pl.BlockSpec((None, 1, D), lambda b, i, k=k: (b, 0, k))
    full = lambda a: pl.BlockSpec(a.shape, lambda b, i, n=a.ndim: (0,) * n)
    tab = lambda a: pl.BlockSpec((tm, a.shape[1]), lambda b, i: (i, 0))
    row = lambda w: pl.BlockSpec((None, tm, w), lambda b, i: (b, i, 0))
    c8, s8, ct, st = tabs
    weights = [p["g_mix"], p["w_a"], p["w_g"], p["g_qnorm"], p["w_q"], p["w_ukp"], p["sel"], p["g_kvnorm"]]
    in_specs = ([row(D), mod_spec(0), mod_spec(1)] + [full(w) for w in weights]
                + [tab(c8), tab(s8), tab(ct), tab(st)])
    out_shape = [jax.ShapeDtypeStruct((NB, H, TT, KC), BF16),
                 jax.ShapeDtypeStruct((NB, TT, KC), BF16),
                 jax.ShapeDtypeStruct((NB, TT, KVL), F32),
                 jax.ShapeDtypeStruct((NB, TT, ROPE), F32),
                 jax.ShapeDtypeStruct((NB, TT, SW), F32),
                 jax.ShapeDtypeStruct((NB, TT, D), F32),
                 jax.ShapeDtypeStruct((NB, TT, D), F32)]
    out_specs = [pl.BlockSpec((None, H, tm, KC), lambda b, i: (b, 0, i, 0)),
                 row(KC), row(KVL), row(ROPE), row(SW), row(D), row(D)]
    return pl.pallas_call(
        functools.partial(_in_proj_kernel, dims),
        grid=grid, in_specs=in_specs, out_specs=out_specs, out_shape=out_shape,
        compiler_params=_cparams("arbitrary", "arbitrary"),
        name="in_proj",
    )(x, mod, mod, *weights, c8, s8, ct, st)


def _attn_prompt_kernel(dims, qi_ref, kj_ref, q_ref, k_ref, o_ref, m_ref, l_ref, acc_ref):
    H, KVL, tq, scale = dims
    s_idx = pl.program_id(1)
    qi = qi_ref[s_idx]
    kj = kj_ref[s_idx]
    M = H * tq

    @pl.when(kj == 0)
    def _():
        m_ref[...] = jnp.full(m_ref.shape, NEG_BIG, F32)
        l_ref[...] = jnp.zeros(l_ref.shape, F32)
        acc_ref[...] = jnp.zeros(acc_ref.shape, F32)

    q = q_ref[...].reshape(M, q_ref.shape[-1])
    k = k_ref[...]
    s = _dot_nt(q, k) * scale

    def update(s):
        m_prev = m_ref[...]
        m_new = jnp.maximum(m_prev, jnp.max(s, axis=-1, keepdims=True))
        alpha = jnp.exp(m_prev - m_new)
        pr = jnp.exp(s - m_new)
        l_ref[...] = alpha * l_ref[...] + jnp.sum(pr, axis=-1, keepdims=True)
        acc_ref[...] = alpha * acc_ref[...] + _dot(pr.astype(BF16), k[:, :KVL])
        m_ref[...] = m_new

    @pl.when(kj < qi)
    def _():
        update(s)

    @pl.when(kj == qi)
    def _():
        tk = k.shape[0]
        t_of_row = lax.broadcasted_iota(jnp.int32, (H, tq, tk), 1).reshape(M, tk)
        key = lax.broadcasted_iota(jnp.int32, (M, tk), 1)
        update(jnp.where(key <= t_of_row, s, NEG_BIG))
        out = acc_ref[...] / l_ref[...]
        for hh in range(H):
            o_ref[:, hh * KVL:(hh + 1) * KVL] = out[hh * tq:(hh + 1) * tq].astype(o_ref.dtype)


def _attn_prompt(q, kcat, KVL, scale, tq):
    NB, H, TT, KC = q.shape
    nq = TT // tq
    pairs = [(i, j) for i in range(nq) for j in range(i + 1)]
    qi = jnp.asarray([a for a, _ in pairs], jnp.int32)
    kj = jnp.asarray([b for _, b in pairs], jnp.int32)
    M = H * tq
    grid_spec = pltpu.PrefetchScalarGridSpec(
        num_scalar_prefetch=2,
        grid=(NB, len(pairs)),
        in_specs=[pl.BlockSpec((None, H, tq, KC), lambda b, s, qi, kj: (b, 0, qi[s], 0)),
                  pl.BlockSpec((None, tq, KC), lambda b, s, qi, kj: (b, kj[s], 0))],
        out_specs=pl.BlockSpec((None, tq, H * KVL), lambda b, s, qi, kj: (b, qi[s], 0)),
        scratch_shapes=[pltpu.VMEM((M, 1), F32), pltpu.VMEM((M, 1), F32), pltpu.VMEM((M, KVL), F32)],
    )
    return pl.pallas_call(
        functools.partial(_attn_prompt_kernel, (H, KVL, tq, scale)),
        grid_spec=grid_spec,
        out_shape=jax.ShapeDtypeStruct((NB, TT, H * KVL), BF16),
        compiler_params=_cparams("arbitrary", "arbitrary"),
        name="attn_prompt",
    )(qi, kj, q, kcat)


def _attn_sample_kernel(dims, pt_ref, q_ref, knew_ref, lat_hbm, krt_hbm, o_ref,
                        m_ref, l_ref, acc_ref, latbuf, krbuf, kb_ref, s_ref, sem):
    H, KVL, ROPE, TD, PGC, SUBP, PAGE, scale, layer = dims
    b = pl.program_id(0)
    c = pl.program_id(1)
    nc = pl.num_programs(1)
    step = b * nc + c
    n_steps = pl.num_programs(0) * nc
    slot = step % 2
    M = H * TD
    SUBK = SUBP * PAGE

    def issue(bb, cc, sl):
        for j in range(PGC):
            pg = pt_ref[bb, cc * PGC + j]
            pltpu.make_async_copy(lat_hbm.at[layer, pg], latbuf.at[sl, j], sem.at[0, sl]).start()
            pltpu.make_async_copy(krt_hbm.at[layer, pg], krbuf.at[sl, j], sem.at[1, sl]).start()

    @pl.when(step == 0)
    def _():
        issue(0, 0, 0)

    @pl.when(step + 1 < n_steps)
    def _():
        wrap = c + 1 == nc
        issue(jnp.where(wrap, b + 1, b), jnp.where(wrap, 0, c + 1), 1 - slot)

    @pl.when(c == 0)
    def _():
        m_ref[...] = jnp.full(m_ref.shape, NEG_BIG, F32)
        l_ref[...] = jnp.zeros(l_ref.shape, F32)
        acc_ref[...] = jnp.zeros(acc_ref.shape, F32)

    pltpu.make_async_copy(lat_hbm.at[layer, pl.ds(0, PGC)], latbuf.at[slot], sem.at[0, slot]).wait()
    pltpu.make_async_copy(krt_hbm.at[layer, pl.ds(0, PGC)], krbuf.at[slot], sem.at[1, slot]).wait()

    q = q_ref[...]
    ql = q[:, :KVL]
    qp = q[:, KVL:KVL + ROPE]

    def update(s, pv_fn):
        m_prev = m_ref[...]
        m_new = jnp.maximum(m_prev, jnp.max(s, axis=-1, keepdims=True))
        alpha = jnp.exp(m_prev - m_new)
        pr = jnp.exp(s - m_new)
        l_ref[...] = alpha * l_ref[...] + jnp.sum(pr, axis=-1, keepdims=True)
        acc_ref[...] = alpha * acc_ref[...] + pv_fn(pr.astype(BF16))
        m_ref[...] = m_new

    for sb in range(PGC // SUBP):
        for j in range(SUBP):
            pgi = sb * SUBP + j
            kb_ref[pgi * PAGE:(pgi + 1) * PAGE, :] = latbuf[slot, pgi].astype(BF16)
        krt = jnp.concatenate([krbuf[slot, sb * SUBP + j].astype(BF16) for j in range(SUBP)], axis=1)
        s_ref[:, sb * SUBK:(sb + 1) * SUBK] = (
            _dot_nt(ql, kb_ref[sb * SUBK:(sb + 1) * SUBK, :]) + _dot(qp, krt)) * scale

    def pv_past(pb):
        out = _dot(pb[:, :SUBK], kb_ref[:SUBK, :])
        for sb in range(1, PGC // SUBP):
            out = out + _dot(pb[:, sb * SUBK:(sb + 1) * SUBK], kb_ref[sb * SUBK:(sb + 1) * SUBK, :])
        return out

    update(s_ref[...], pv_past)

    @pl.when(c == nc - 1)
    def _():
        kn = knew_ref[...]
        KN = kn.shape[0]
        sn = _dot_nt(q, kn) * scale
        t_of_row = lax.broadcasted_iota(jnp.int32, (H, TD, KN), 1).reshape(M, KN)
        key = lax.broadcasted_iota(jnp.int32, (M, KN), 1)
        update(jnp.where(key <= t_of_row, sn, NEG_BIG), lambda pb: _dot(pb, kn[:, :KVL]))
        o_ref[...] = (acc_ref[...] / l_ref[...]).astype(o_ref.dtype)


def _attn_sample(q, knew, cache_latent, cache_krope_t, page_table, layer, scale, H, TD):
    NB, M, KC = q.shape
    _, _, PAGE, KVL = cache_latent.shape
    ROPE = cache_krope_t.shape[2]
    n_pages = page_table.shape[1]
    PGC = min(PAGES_PER_STEP, n_pages)
    SUBP = min(PAGES_PER_DOT, PGC)
    assert n_pages % PGC == 0 and PGC % SUBP == 0
    KN = knew.shape[1]
    grid_spec = pltpu.PrefetchScalarGridSpec(
        num_scalar_prefetch=1, grid=(NB, n_pages // PGC),
        in_specs=[pl.BlockSpec((None, M, KC), lambda b, c, pt: (b, 0, 0)),
                  pl.BlockSpec((None, KN, KC), lambda b, c, pt: (b, 0, 0)),
                  pl.BlockSpec(memory_space=pl.ANY), pl.BlockSpec(memory_space=pl.ANY)],
        out_specs=pl.BlockSpec((None, M, KVL), lambda b, c, pt: (b, 0, 0)),
        scratch_shapes=[pltpu.VMEM((M, 1), F32), pltpu.VMEM((M, 1), F32), pltpu.VMEM((M, KVL), F32),
                        pltpu.VMEM((2, PGC, PAGE, KVL), F32), pltpu.VMEM((2, PGC, ROPE, PAGE), F32),
                        pltpu.VMEM((PGC * PAGE, KVL), BF16), pltpu.VMEM((M, PGC * PAGE), F32),
                        pltpu.SemaphoreType.DMA((2, 2))],
    )
    return pl.pallas_call(
        functools.partial(_attn_sample_kernel, (H, KVL, ROPE, TD, PGC, SUBP, PAGE, scale, layer)),
        grid_spec=grid_spec,
        out_shape=jax.ShapeDtypeStruct((NB, M, KVL), BF16),
        compiler_params=_cparams("arbitrary", "arbitrary"),
        name="attn_sample",
    )(page_table, q, knew, cache_latent, cache_krope_t)


def _s5_kernel(dims, u_ref, h0r_ref, h0i_ref, ar_ref, ai_ref, bre_ref, bim_ref, cre_ref, cim_ref,
               d_ref, wglu_ref, bglu_ref, o_ref, fr_ref, fi_ref, sr_ref, si_ref, hr_ref, hi_ref):
    NB, tt, SW, NS, CW, RG = dims
    i = pl.program_id(0)

    @pl.when(i == 0)
    def _():
        sr_ref[...] = h0r_ref[...]
        si_ref[...] = h0i_ref[...]

    u = u_ref[...]
    ub = u.astype(BF16)
    nb_cols = bre_ref.shape[2]
    for n in range(NS // nb_cols):
        k0 = (n * nb_cols // (NS // SW)) // LANES * LANES
        uu = ub[:, k0:k0 + LANES]
        hr_ref[:, n * nb_cols:(n + 1) * nb_cols] = _dot(uu, bre_ref[n])
        hi_ref[:, n * nb_cols:(n + 1) * nb_cols] = _dot(uu, bim_ref[n])

    for cc in range(NS // CW):
        cols = slice(cc * CW, (cc + 1) * CW)
        ar = ar_ref[:, cols]
        ai = ai_ref[:, cols]
        for rg in range(NB // RG):
            rows = slice(rg * RG, (rg + 1) * RG)

            def body(t, carry):
                h_re, h_im = carry
                r0 = pl.multiple_of(t * NB + rg * RG, SUBLANES)
                n_re = ar * h_re - ai * h_im + hr_ref[pl.ds(r0, RG), cols]
                n_im = ar * h_im + ai * h_re + hi_ref[pl.ds(r0, RG), cols]
                hr_ref[pl.ds(r0, RG), cols] = n_re
                hi_ref[pl.ds(r0, RG), cols] = n_im
                return n_re, n_im

            h_re, h_im = lax.fori_loop(0, tt, body, (sr_ref[rows, cols], si_ref[rows, cols]),
                                       unroll=min(tt, 4))
            sr_ref[rows, cols] = h_re
            si_ref[rows, cols] = h_im

    kc = cre_ref.shape[1]
    ys = []
    for j in range(SW // cre_ref.shape[2]):
        hb_re = hr_ref[:, j * kc:(j + 1) * kc].astype(BF16)
        hb_im = hi_ref[:, j * kc:(j + 1) * kc].astype(BF16)
        ys.append(_dot(hb_re, cre_ref[j]) - _dot(hb_im, cim_ref[j]))
    y = jnp.concatenate(ys, axis=-1) + d_ref[...] * u
    g = jax.nn.gelu(y)
    o_ref[...] = (g * jax.nn.sigmoid(_dot(g.astype(BF16), wglu_ref[...]) + bglu_ref[...])).astype(o_ref.dtype)

    @pl.when(i == pl.num_programs(0) - 1)
    def _():
        fr_ref[...] = sr_ref[...]
        fi_ref[...] = si_ref[...]


def _s5(u_tm, h0_re, h0_im, p, NB, tt):
    R, SW = u_tm.shape
    NS = h0_re.shape[1]
    rows = tt * NB
    CW = 512
    RG = 16
    assert NB % RG == 0 and NS % CW == 0 and R % rows == 0
    full = lambda a: pl.BlockSpec(a.shape, lambda i, n=a.ndim: (0,) * n)
    ws = [p["abar_re"], p["abar_im"], p["b_re"], p["b_im"], p["c_re"], p["c_im"], p["ssm_d"], p["w_glu"], p["b_glu"]]
    return pl.pallas_call(
        functools.partial(_s5_kernel, (NB, tt, SW, NS, CW, RG)),
        grid=(R // rows,),
        in_specs=[pl.BlockSpec((rows, SW), lambda i: (i, 0)), full(h0_re), full(h0_im)] + [full(w) for w in ws],
        out_specs=[pl.BlockSpec((rows, SW), lambda i: (i, 0)),
                   pl.BlockSpec((NB, NS), lambda i: (0, 0)), pl.BlockSpec((NB, NS), lambda i: (0, 0))],
        out_shape=[jax.ShapeDtypeStruct((R, SW), BF16),
                   jax.ShapeDtypeStruct((NB, NS), F32), jax.ShapeDtypeStruct((NB, NS), F32)],
        scratch_shapes=[pltpu.VMEM((NB, NS), F32), pltpu.VMEM((NB, NS), F32),
                        pltpu.VMEM((rows, NS), F32), pltpu.VMEM((rows, NS), F32)],
        compiler_params=_cparams("arbitrary"),
        name="s5_scan",
    )(u_tm, h0_re, h0_im, *ws)


def _merge_kernel(dims, x_ref, ol_ref, os_ref, ga_ref, gs_ref, gta_ref, shm_ref, scm_ref,
                  wuv_ref, wap_ref, wsp_ref, wout_ref, gffn_ref, wrt_ref,
                  xo_ref, h2_ref, lg_ref):
    n_slab, slab_k = dims
    ol = ol_ref[...]
    o = jnp.concatenate([_dot(ol[:, p * slab_k:(p + 1) * slab_k], wuv_ref[p]) for p in range(n_slab)], axis=-1)
    a = _dot(o.astype(BF16), wap_ref[...])
    s = _dot(os_ref[...], wsp_ref[...])
    merged = (ga_ref[...] * a + gs_ref[...] * s).astype(BF16)
    m = _dot(merged, wout_ref[...])
    x = x_ref[...] + gta_ref[...] * m
    xo_ref[...] = x
    h2 = (_rms(x, gffn_ref[...]) * (1.0 + scm_ref[...]) + shm_ref[...]).astype(BF16)
    h2_ref[...] = _pack_bf16_pairs(h2)
    lg_ref[...] = _dot_nt(wrt_ref[...], h2)


def _merge(x, o_lat, o_s, ga, gs, mod, per_row_mod, p, tm):
    NB, TT, D = x.shape
    HK = o_lat.shape[-1]
    SW = o_s.shape[-1]
    E = p["w_router_t"].shape[0]
    n_slab = p["w_uvp"].shape[0]
    slab_k = p["w_uvp"].shape[1]
    if per_row_mod:
        mod_spec = lambda k: pl.BlockSpec((None, tm, D), lambda b, i, k=k: (b, i, k))
    else:
        mod_spec = lambda k: pl.BlockSpec((None, 1, D), lambda b, i, k=k: (b, 0, k))
    row = lambda w: pl.BlockSpec((None, tm, w), lambda b, i: (b, i, 0))
    full = lambda a: pl.BlockSpec(a.shape, lambda b, i, n=a.ndim: (0,) * n)
    ws = [p["w_uvp"], p["w_attn_proj"], p["w_ssm_proj"], p["w_out"], p["g_ffn"], p["w_router_t"]]
    return pl.pallas_call(
        functools.partial(_merge_kernel, (n_slab, slab_k)),
        grid=(NB, TT // tm),
        in_specs=[row(D), row(HK), row(SW), row(D), row(D), mod_spec(2), mod_spec(3), mod_spec(4)]
                 + [full(w) for w in ws],
        out_specs=[row(D), row(D // 2), pl.BlockSpec((None, E, tm), lambda b, i: (b, 0, i))],
        out_shape=[jax.ShapeDtypeStruct((NB, TT, D), F32), jax.ShapeDtypeStruct((NB, TT, D // 2), jnp.uint32),
                   jax.ShapeDtypeStruct((NB, E, TT), F32)],
        compiler_params=_cparams("arbitrary", "arbitrary"),
        name="merge_out_proj",
    )(x, o_lat, o_s, ga, gs, mod, mod, mod, *ws)


def _first_index(mask, iota, big):
    return jnp.min(jnp.where(mask, iota, big), axis=0, keepdims=True)


def _route_kernel(dims, lg_ref, bias_ref, tri_ref, idx_ref, wts_ref, rank_ref, cnt_ref, base_ref):
    E, NG = dims
    GS = E // NG
    first = (pl.program_id(0) == 0) & (pl.program_id(1) == 0)

    @pl.when(first)
    def _():
        base_ref[...] = jnp.zeros(base_ref.shape, F32)

    scores = jax.nn.sigmoid(lg_ref[...])
    choice = scores + bias_ref[...]
    tm = scores.shape[1]
    eio = lax.broadcasted_iota(jnp.int32, (E, tm), 0).astype(F32)
    gio = lax.broadcasted_iota(jnp.int32, (GS, tm), 0).astype(F32)
    gscore = []
    for g in range(NG):
        cg = choice[g * GS:(g + 1) * GS]
        m1 = jnp.max(cg, axis=0, keepdims=True)
        i1 = _first_index(cg == m1, gio, float(GS))
        m2 = jnp.max(jnp.where(gio == i1, -jnp.inf, cg), axis=0, keepdims=True)
        gscore.append(m1 + m2)
    gscore = jnp.concatenate(gscore, axis=0)
    nio = lax.broadcasted_iota(jnp.int32, (NG, tm), 0).astype(F32)
    gsel = jnp.zeros((NG, tm), F32)
    rem = gscore
    for _ in range(TOPK_GROUPS):
        mg = jnp.max(rem, axis=0, keepdims=True)
        ig = _first_index(rem == mg, nio, float(NG))
        hit = nio == ig
        gsel = jnp.where(hit, 1.0, gsel)
        rem = jnp.where(hit, -jnp.inf, rem)
    allowed = jnp.concatenate(
        [jnp.broadcast_to(gsel[g:g + 1], (GS, tm)) for g in range(NG)], axis=0) > 0.5
    rem = jnp.where(allowed, choice, -jnp.inf)
    picks, vals = [], []
    chosen = jnp.zeros((E, tm), F32)
    for _ in range(TOP_K):
        mk = jnp.max(rem, axis=0, keepdims=True)
        ik = _first_index(rem == mk, eio, float(E))
        hit = eio == ik
        picks.append(ik)
        vals.append(jnp.sum(jnp.where(hit, scores, 0.0), axis=0, keepdims=True))
        chosen = chosen + hit.astype(F32)
        rem = jnp.where(hit, -jnp.inf, rem)
    total = vals[0]
    for v in vals[1:]:
        total = total + v
    prefix = _dot(chosen.astype(BF16), tri_ref[...]) + base_ref[...]
    ranks = [jnp.sum(jnp.where(eio == ik, prefix, 0.0), axis=0, keepdims=True) for ik in picks]
    base_ref[...] = base_ref[...] + jnp.sum(chosen, axis=1, keepdims=True)
    pad_i = [jnp.zeros((1, tm), jnp.int32)] * (TOP_K_PAD - TOP_K)
    pad_f = [jnp.zeros((1, tm), F32)] * (TOP_K_PAD - TOP_K)
    idx_ref[...] = jnp.concatenate([ik.astype(jnp.int32) for ik in picks] + pad_i, axis=0)
    wts_ref[...] = jnp.concatenate([v / total * ROUTED_SCALE for v in vals] + pad_f, axis=0)
    rank_ref[...] = jnp.concatenate([r.astype(jnp.int32) for r in ranks] + pad_i, axis=0)
    cnt_ref[...] = jnp.broadcast_to(base_ref[...], cnt_ref.shape)


def _route(logits_t, bias, tri, tm):
    NB, E, TT = logits_t.shape
    blk = lambda: pl.BlockSpec((None, TOP_K_PAD, tm), lambda b, i: (b, 0, i))
    return pl.pallas_call(
        functools.partial(_route_kernel, (E, N_EXPERT_GROUPS)),
        grid=(NB, TT // tm),
        in_specs=[pl.BlockSpec((None, E, tm), lambda b, i: (b, 0, i)),
                  pl.BlockSpec((E, 1), lambda b, i: (0, 0)),
                  pl.BlockSpec((tm, tm), lambda b, i: (0, 0))],
        out_specs=[blk(), blk(), blk(), pl.BlockSpec((E, LANES), lambda b, i: (0, 0))],
        out_shape=[jax.ShapeDtypeStruct((NB, TOP_K_PAD, TT), jnp.int32),
                   jax.ShapeDtypeStruct((NB, TOP_K_PAD, TT), F32),
                   jax.ShapeDtypeStruct((NB, TOP_K_PAD, TT), jnp.int32),
                   jax.ShapeDtypeStruct((E, LANES), F32)],
        scratch_shapes=[pltpu.VMEM((E, 1), F32)],
        compiler_params=_cparams("arbitrary", "arbitrary"),
        name="route_topk",
    )(logits_t, bias, tri)


def _gather_rows(idx_hbm, src_hbm, idx_smem, buf, isem, rsem, step, n_steps, rows):
    slot = step % 2
    nslot = 1 - slot

    def idx_copy(s, sl):
        return pltpu.make_async_copy(idx_hbm.at[s], idx_smem.at[sl], isem.at[sl])

    def issue(sl):
        for r in range(rows):
            pltpu.make_async_copy(src_hbm.at[idx_smem[sl, r]], buf.at[sl, r], rsem.at[sl]).start(priority=r % 2)

    @pl.when(step == 0)
    def _():
        c = idx_copy(0, 0)
        c.start()
        c.wait()
        issue(0)

        @pl.when(n_steps > 1)
        def _():
            idx_copy(1, 1).start()

    @pl.when(step + 1 < n_steps)
    def _():
        idx_copy(step + 1, nslot).wait()
        issue(nslot)

    pltpu.make_async_copy(src_hbm.at[pl.ds(0, rows)], buf.at[slot], rsem.at[slot]).wait()

    @pl.when(step + 2 < n_steps)
    def _():
        idx_copy(step + 2, slot).start()

    return slot


def _dispatch_kernel(dims, zb_ref, nvb_ref, dest_hbm, x_ref, xs_hbm, idx_smem, zbuf, isem, rsem, zsem):
    tm, E, rows, n_blocks = dims
    step = pl.program_id(0)
    n_steps = pl.num_programs(0)
    slot = step % 2

    def idx_copy(s, sl):
        return pltpu.make_async_copy(dest_hbm.at[s], idx_smem.at[sl], isem.at[sl])

    def zero_copy(blk):
        return pltpu.make_async_copy(zbuf, xs_hbm.at[pl.ds(pl.multiple_of(blk * rows, rows), rows)], zsem)

    @pl.when(step == 0)
    def _():
        idx_copy(0, 0).start()
        zbuf[...] = jnp.zeros(zbuf.shape, zbuf.dtype)
        for e in range(E):
            @pl.when(zb_ref[e] >= 0)
            def _():
                zero_copy(zb_ref[e]).start()
        lax.fori_loop(nvb_ref[0], n_blocks, lambda i, c: (zero_copy(i).start(), c)[1], 0)
        for e in range(E):
            @pl.when(zb_ref[e] >= 0)
            def _():
                zero_copy(zb_ref[e]).wait()
        lax.fori_loop(nvb_ref[0], n_blocks, lambda i, c: (zero_copy(i).wait(), c)[1], 0)

    idx_copy(step, slot).wait()

    @pl.when(step + 1 < n_steps)
    def _():
        idx_copy(step + 1, 1 - slot).start()

    def row_copy(r):
        return pltpu.make_async_copy(x_ref.at[r % tm], xs_hbm.at[idx_smem[slot, r]], rsem)

    for r in range(TOP_K * tm):
        row_copy(r).start(priority=r % 2)
    for k in range(TOP_K):
        pltpu.make_async_copy(x_ref, xs_hbm.at[pl.ds(0, tm)], rsem).wait()


def _dispatch(last_block, n_valid, dest, h2p, n_blocks, rows):
    n_steps, per_step = dest.shape
    tm = per_step // TOP_K
    N, W = h2p.shape
    E = last_block.shape[0]
    grid_spec = pltpu.PrefetchScalarGridSpec(
        num_scalar_prefetch=2, grid=(n_steps,),
        in_specs=[pl.BlockSpec(memory_space=pl.ANY), pl.BlockSpec((tm, W), lambda i, zb, nv: (i, 0))],
        out_specs=pl.BlockSpec(memory_space=pl.ANY),
        scratch_shapes=[pltpu.SMEM((2, per_step), jnp.int32), pltpu.VMEM((rows, W), jnp.uint32),
                        pltpu.SemaphoreType.DMA((2,)), pltpu.SemaphoreType.DMA(()), pltpu.SemaphoreType.DMA(())],
    )
    return pl.pallas_call(
        functools.partial(_dispatch_kernel, (tm, E, rows, n_blocks)), grid_spec=grid_spec,
        out_shape=jax.ShapeDtypeStruct((n_blocks * rows, W), jnp.uint32),
        compiler_params=_cparams("arbitrary"),
        name="moe_dispatch",
    )(last_block, n_valid, dest, h2p)


def _experts_kernel(be_ref, nvb_ref, x_ref, w1_ref, w3_ref, w2_ref, y_ref):
    @pl.when(pl.program_id(0) < nvb_ref[0])
    def _():
        y_ref[...] = _swiglu_packed(x_ref[...], w1_ref, w3_ref, w2_ref)

    @pl.when(pl.program_id(0) >= nvb_ref[0])
    def _():
        y_ref[...] = jnp.zeros(y_ref.shape, y_ref.dtype)


def _experts(block_e, n_valid, xs, w1, w3, w2, rows):
    n_blocks = block_e.shape[0]
    W = xs.shape[1]
    _, D, F = w1.shape
    blk = lambda i, be, nv: jnp.minimum(i, nv[0] - 1)
    grid_spec = pltpu.PrefetchScalarGridSpec(
        num_scalar_prefetch=2, grid=(n_blocks,),
        in_specs=[pl.BlockSpec((rows, W), lambda i, be, nv: (blk(i, be, nv), 0)),
                  pl.BlockSpec((None, D, F), lambda i, be, nv: (be[blk(i, be, nv)], 0, 0)),
                  pl.BlockSpec((None, D, F), lambda i, be, nv: (be[blk(i, be, nv)], 0, 0)),
                  pl.BlockSpec((None, F, D), lambda i, be, nv: (be[blk(i, be, nv)], 0, 0))],
        out_specs=pl.BlockSpec((rows, D), lambda i, be, nv: (i, 0)),
    )
    return pl.pallas_call(
        _experts_kernel, grid_spec=grid_spec,
        out_shape=jax.ShapeDtypeStruct((n_blocks * rows, D), F32),
        compiler_params=_cparams("arbitrary"),
        name="routed_experts",
    )(block_e, n_valid, xs, w1, w3, w2)


def _combine_kernel(dims, pos_hbm, y_hbm, x_ref, h2_ref, wts_ref, gtm_ref, w1_ref, w3_ref, w2_ref, gfin_ref,
                    xo_ref, idx_smem, ybuf, isem, rsem):
    tm, final = dims
    step = pl.program_id(0)
    slot = _gather_rows(pos_hbm, y_hbm, idx_smem, ybuf, isem, rsem, step, pl.num_programs(0), TOP_K * tm)
    wts = wts_ref[...]
    routed = wts[:, 0:1] * ybuf[slot, 0:tm]
    for k in range(1, TOP_K):
        routed = routed + wts[:, k:k + 1] * ybuf[slot, k * tm:(k + 1) * tm]
    shared = _swiglu_packed(h2_ref[...], w1_ref, w3_ref, w2_ref)
    x = x_ref[...] + gtm_ref[...] * (routed + shared)
    xo_ref[...] = _rms(x, gfin_ref[...]) if final else x


def _combine(pos, y_sorted, x, h2, wts, gtm, per_row_mod, p, g_final, final, tm, tokens_per_mod):
    N, D = x.shape
    n_steps = N // tm
    if per_row_mod:
        gt_spec = pl.BlockSpec((None, tm, D), lambda i: (0, i, 5))
    else:
        per = tokens_per_mod // tm
        gt_spec = pl.BlockSpec((None, 1, D), lambda i: (i // per, 0, 5))
    full = lambda a: pl.BlockSpec(a.shape, lambda i, n=a.ndim: (0,) * n)
    row = lambda w: pl.BlockSpec((tm, w), lambda i: (i, 0))
    return pl.pallas_call(
        functools.partial(_combine_kernel, (tm, final)),
        grid=(n_steps,),
        in_specs=[pl.BlockSpec(memory_space=pl.ANY), pl.BlockSpec(memory_space=pl.ANY),
                  row(D), row(D // 2), row(TOP_K_PAD), gt_spec,
                  full(p["w_s1"]), full(p["w_s3"]), full(p["w_s2"]), full(g_final)],
        out_specs=row(D),
        out_shape=jax.ShapeDtypeStruct((N, D), F32),
        scratch_shapes=[pltpu.SMEM((2, TOP_K * tm), jnp.int32), pltpu.VMEM((2, TOP_K * tm, D), F32),
                        pltpu.SemaphoreType.DMA((2,)), pltpu.SemaphoreType.DMA((2,))],
        compiler_params=_cparams("arbitrary"),
        name="combine_shared",
    )(pos, y_sorted, x, h2, wts, gtm, p["w_s1"], p["w_s3"], p["w_s2"], g_final)


def _prep_layer(l, w, cfg):
    H, QL, KVL, ROPE, NOPE, SW, D, G, P, C = cfg
    half = ROPE // 2
    w_in = w["w_in"][l]
    o_kpe = QL + KVL
    o_u = o_kpe + ROPE
    o_ga = o_u + SW
    k1 = w_in[:, o_kpe:o_kpe + half]
    k2 = w_in[:, o_kpe + half:o_kpe + ROPE]
    zpad = jnp.zeros((D, LANES - ROPE), F32)
    w_a = jnp.concatenate([w_in[:, :o_kpe], k1, k2, zpad, k2, k1, zpad, w_in[:, o_u:o_ga]], axis=1)
    w_uq = w["w_uq"][l]
    w_q = jnp.concatenate([w_uq[:, :, :NOPE].reshape(QL, H * NOPE),
                           w_uq[:, :, NOPE:NOPE + half].reshape(QL, H * half),
                           w_uq[:, :, NOPE + half:].reshape(QL, H * half)], axis=1)
    hps = LANES // NOPE
    ukt = jnp.swapaxes(w["w_uk"][l], 1, 2)
    w_ukp = jnp.zeros((H // hps, LANES, hps * KVL), F32)
    for hh in range(H):
        pp, j = divmod(hh, hps)
        w_ukp = w_ukp.at[pp, j * NOPE:(j + 1) * NOPE, j * KVL:(j + 1) * KVL].set(ukt[hh])
    uv = w["w_uv"][l]
    V = uv.shape[-1]
    hpv = LANES // V
    w_uvp = jnp.zeros((H // hpv, hpv * KVL, LANES), F32)
    for hh in range(H):
        pp, j = divmod(hh, hpv)
        w_uvp = w_uvp.at[pp, j * KVL:(j + 1) * KVL, j * V:(j + 1) * V].set(uv[hh])
    a_re, a_im = w["ssm_a_re"][l], w["ssm_a_im"][l]
    dt = jnp.exp(w["ssm_log_dt"][l])[:, None]
    mag = jnp.exp(dt * a_re)
    abar_re, abar_im = mag * jnp.cos(dt * a_im), mag * jnp.sin(dt * a_im)
    den = a_re * a_re + a_im * a_im
    coef_re = ((abar_re - 1.0) * a_re + abar_im * a_im) / den
    coef_im = (abar_im * a_re - (abar_re - 1.0) * a_im) / den
    b_re, b_im = w["ssm_b_re"][l], w["ssm_b_im"][l]
    bbar_re = coef_re[..., None] * b_re - coef_im[..., None] * b_im
    bbar_im = coef_re[..., None] * b_im + coef_im[..., None] * b_re
    NS = G * P
    NBC = 2 * LANES
    gpb = NBC // P

    def in_blocks(bb):
        blk = jnp.zeros((NS // NBC, LANES, NBC), F32)
        for g in range(G):
            n, j = divmod(g, gpb)
            r0 = (g * C) % LANES
            blk = blk.at[n, r0:r0 + C, j * P:(j + 1) * P].set(bb[g].T)
        return blk.astype(BF16)

    OC = 2 * LANES
    gpo = OC // C

    def out_blocks(cc):
        blk = jnp.zeros((SW // OC, gpo * P, OC), F32)
        for g in range(G):
            n, j = divmod(g, gpo)
            blk = blk.at[n, j * P:(j + 1) * P, j * C:(j + 1) * C].set(cc[g].T)
        return blk.astype(BF16)

    return dict(
        g_mix=w["g_mix"][l][None], w_a=w_a.astype(BF16), w_g=w_in[:, o_ga:].astype(BF16),
        g_qnorm=w["g_qnorm"][l][None], w_q=w_q.astype(BF16), w_ukp=w_ukp.astype(BF16),
        g_kvnorm=w["g_kvnorm"][l][None], w_uvp=w_uvp.astype(BF16),
        abar_re=abar_re.reshape(1, NS), abar_im=abar_im.reshape(1, NS),
        b_re=in_blocks(bbar_re), b_im=in_blocks(bbar_im),
        c_re=out_blocks(w["ssm_c_re"][l]), c_im=out_blocks(w["ssm_c_im"][l]),
        ssm_d=w["ssm_d"][l].reshape(1, SW), w_glu=w["w_glu"][l].astype(BF16), b_glu=w["b_glu"][l][None],
        w_attn_proj=w["w_attn_proj"][l].astype(BF16), w_ssm_proj=w["w_ssm_proj"][l].astype(BF16),
        w_out=w["w_out"][l].astype(BF16), g_ffn=w["g_ffn"][l][None],
        w_router_t=w["w_router"][l].T.astype(BF16), router_bias=w["router_bias"][l][:, None],
        w_e1=w["w_e1"][l].astype(BF16), w_e3=w["w_e3"][l].astype(BF16), w_e2=w["w_e2"][l].astype(BF16),
        w_s1=w["w_s1"][l].astype(BF16), w_s3=w["w_s3"][l].astype(BF16), w_s2=w["w_s2"][l].astype(BF16),
    )


def _rope_tabs(pos, H, ROPE):
    half = ROPE // 2
    inv = ROPE_THETA ** (-jnp.arange(0, ROPE, 2, dtype=F32) / ROPE)
    ang = pos.astype(F32)[:, None] * inv[None, :]
    cos, sin = jnp.cos(ang), jnp.sin(ang)
    T = pos.shape[0]
    zpad = jnp.zeros((T, LANES - ROPE), F32)
    return (jnp.tile(cos, (1, H)), jnp.tile(sin, (1, H)),
            jnp.concatenate([cos, cos, zpad], axis=1), jnp.concatenate([-sin, sin, zpad], axis=1))


def _head_select(H, ROPE):
    half = ROPE // 2
    sel = np.zeros((2 * H * half, H * LANES), np.float32)
    for hh in range(H):
        for i in range(half):
            sel[hh * half + i, hh * LANES + i] = 1.0
            sel[H * half + hh * half + i, hh * LANES + half + i] = 1.0
    return jnp.asarray(sel, BF16)


def _moe(x2, h2p, logits_t, gtm, per_row_mod, tokens_per_mod, p, tri, g_final, final, tm_route, rows):
    N, D = x2.shape
    NB, E, TT = logits_t.shape
    tm = min(MOE_TOKENS, N)
    idx_t, wts_t, rank_t, cnt = _route(logits_t, p["router_bias"], tri, tm_route)
    to_rows = lambda a: jnp.swapaxes(a, 1, 2).reshape(N, TOP_K_PAD)
    idx, wts, rank = to_rows(idx_t), to_rows(wts_t), to_rows(rank_t)
    counts = cnt[:, 0].astype(jnp.int32)
    padded = (counts + rows - 1) // rows * rows
    pad_end = jnp.cumsum(padded)
    pad_start = pad_end - padded
    n_blocks = -(-(N * TOP_K) // rows) + E
    dest = pad_start[idx[:, :TOP_K]] + rank[:, :TOP_K]
    pos = jnp.swapaxes(dest.reshape(N // tm, tm, TOP_K), 1, 2).reshape(N // tm, TOP_K * tm)
    block_start = jnp.arange(n_blocks, dtype=jnp.int32) * rows
    block_e = jnp.minimum(jnp.sum((pad_end[None, :] <= block_start[:, None]).astype(jnp.int32), axis=1), E - 1)
    n_valid = (pad_end[E - 1:] // rows).astype(jnp.int32)
    last_block = jnp.where(padded > 0, pad_end // rows - 1, -1).astype(jnp.int32)
    xs = _dispatch(last_block, n_valid, pos, h2p, n_blocks, rows)
    y_sorted = _experts(block_e, n_valid, xs, p["w_e1"], p["w_e3"], p["w_e2"], rows)
    return _combine(pos, y_sorted, x2, h2p, wts, gtm, per_row_mod, p, g_final, final, tm, tokens_per_mod)


def kernel(x_prompt, x_sample, cache_latent, cache_krope, state_ssm_re, state_ssm_im, page_table, c_prompt, c_sample, w_ada, b_ada, g_mix, w_in, g_qnorm, w_uq, g_kvnorm, w_uk, w_uv, ssm_a_re, ssm_a_im, ssm_log_dt, ssm_b_re, ssm_b_im, ssm_c_re, ssm_c_im, ssm_d, w_glu, b_glu, w_attn_proj, w_ssm_proj, w_out, g_ffn, w_router, router_bias, w_e1, w_e3, w_e2, w_s1, w_s3, w_s2, g_final):
    w = dict(w_in=w_in, g_mix=g_mix, g_qnorm=g_qnorm, w_uq=w_uq, g_kvnorm=g_kvnorm, w_uk=w_uk, w_uv=w_uv,
             ssm_a_re=ssm_a_re, ssm_a_im=ssm_a_im, ssm_log_dt=ssm_log_dt, ssm_b_re=ssm_b_re, ssm_b_im=ssm_b_im,
             ssm_c_re=ssm_c_re, ssm_c_im=ssm_c_im, ssm_d=ssm_d, w_glu=w_glu, b_glu=b_glu,
             w_attn_proj=w_attn_proj, w_ssm_proj=w_ssm_proj, w_out=w_out, g_ffn=g_ffn, w_router=w_router,
             router_bias=router_bias, w_e1=w_e1, w_e3=w_e3, w_e2=w_e2, w_s1=w_s1, w_s3=w_s3, w_s2=w_s2)
    BP, TP, D = x_prompt.shape
    BS, TS, _ = x_sample.shape
    L = w_in.shape[0]
    QL, H, QKD = w_uq.shape[1:]
    KVL = cache_latent.shape[-1]
    ROPE = cache_krope.shape[-1]
    PAGE = cache_latent.shape[2]
    NOPE = QKD - ROPE
    G, P = ssm_a_re.shape[1:]
    C = SSM_GROUP
    SW = G * C
    NS = G * P
    E = w_router.shape[-1]
    past_len = page_table.shape[1] * PAGE
    scale = float(QKD) ** -0.5
    cfg = (H, QL, KVL, ROPE, NOPE, SW, D, G, P, C)
    dims = (H, QL, KVL, ROPE, NOPE, SW, D)
    assert H * ROPE // 2 == LANES and LANES % NOPE == 0 and KVL % LANES == 0

    NSAMP = BS * TS
    tm_p = min(256, TP)
    tm_s = min(256, NSAMP)
    tq = min(256, TP)
    tt_p = max(1, min(TP, 512 // BP))
    rows_for = lambda n: int(min(512, max(128, pl.next_power_of_2(n * TOP_K // E))))
    rows_p, rows_s = rows_for(BP * TP), rows_for(NSAMP)
    cache_krope_t = jnp.swapaxes(cache_krope, 2, 3)

    mod_all = _modulation(jnp.concatenate([c_prompt, c_sample], axis=0), w_ada.astype(BF16), b_ada[:, None, :])
    tabs_p = _rope_tabs(jnp.arange(TP, dtype=jnp.int32), H, ROPE)
    tabs_s1 = _rope_tabs(past_len + jnp.arange(TS, dtype=jnp.int32), H, ROPE)
    tabs_s = tuple(jnp.tile(t, (BS, 1)) for t in tabs_s1)
    sel = _head_select(H, ROPE)
    tri_p = jnp.asarray(np.triu(np.ones((tm_p, tm_p), np.float32), 1), BF16)
    tri_s = jnp.asarray(np.triu(np.ones((tm_s, tm_s), np.float32), 1), BF16)
    g_fin = g_final[None]
    zeros_state = jnp.zeros((BP, NS), F32)

    xp = x_prompt
    xs = x_sample.reshape(1, NSAMP, D)
    outs = {k: [] for k in ("lat_p", "kpe_p", "sre_p", "sim_p", "lat_s", "kpe_s", "sre_s", "sim_s")}
    for l in range(L):
        p = _prep_layer(l, w, cfg)
        p["sel"] = sel
        final = l == L - 1
        mod_p = mod_all[l, :BP][:, None, :]
        mod_s = jnp.repeat(mod_all[l, BP:], TS, axis=0)[None]

        q, kcat, kv, kpe, u, ga, gs = _in_proj(xp, mod_p, False, p, tabs_p, dims, tm_p)
        outs["lat_p"].append(kv)
        outs["kpe_p"].append(kpe)
        o_lat = _attn_prompt(q, kcat, KVL, scale, tq)
        u_tm = jnp.swapaxes(u, 0, 1).reshape(TP * BP, SW)
        o_s_tm, f_re, f_im = _s5(u_tm, zeros_state, zeros_state, p, BP, tt_p)
        outs["sre_p"].append(f_re.reshape(BP, G, P))
        outs["sim_p"].append(f_im.reshape(BP, G, P))
        o_s = jnp.swapaxes(o_s_tm.reshape(TP, BP, SW), 0, 1)
        x1, h2, lg = _merge(xp, o_lat, o_s, ga, gs, mod_p, False, p, tm_p)
        xp = _moe(x1.reshape(BP * TP, D), h2.reshape(BP * TP, D // 2), lg, mod_p, False, TP, p, tri_p,
                  g_fin, final, tm_p, rows_p).reshape(BP, TP, D)

        q, kcat, kv, kpe, u, ga, gs = _in_proj(xs, mod_s, True, p, tabs_s, dims, tm_s)
        outs["lat_s"].append(kv.reshape(BS, TS, KVL))
        outs["kpe_s"].append(kpe.reshape(BS, TS, ROPE))
        KC = KVL + LANES
        q_s = jnp.swapaxes(q.reshape(H, BS, TS, KC), 0, 1).reshape(BS, H * TS, KC)
        knew = jnp.pad(kcat.reshape(BS, TS, KC), ((0, 0), (0, NEW_KEY_ROWS - TS), (0, 0)))
        o_lat_s = _attn_sample(q_s, knew, cache_latent, cache_krope_t, page_table, l, scale, H, TS)
        o_lat_s = jnp.swapaxes(o_lat_s.reshape(BS, H, TS, KVL), 1, 2).reshape(1, NSAMP, H * KVL)
        u_tm = jnp.swapaxes(u.reshape(BS, TS, SW), 0, 1).reshape(TS * BS, SW)
        o_s_tm, f_re, f_im = _s5(u_tm, state_ssm_re[l].reshape(BS, NS), state_ssm_im[l].reshape(BS, NS), p, BS, TS)
        outs["sre_s"].append(f_re.reshape(BS, G, P))
        outs["sim_s"].append(f_im.reshape(BS, G, P))
        o_s = jnp.swapaxes(o_s_tm.reshape(TS, BS, SW), 0, 1).reshape(1, NSAMP, SW)
        x1, h2, lg = _merge(xs, o_lat_s, o_s, ga, gs, mod_s, True, p, tm_s)
        xs = _moe(x1.reshape(NSAMP, D), h2.reshape(NSAMP, D // 2), lg, mod_s, True, NSAMP, p, tri_s,
                  g_fin, final, tm_s, rows_s).reshape(1, NSAMP, D)

    st = lambda k: jnp.stack(outs[k])
    return (xp, xs.reshape(BS, TS, D), st("lat_p"), st("kpe_p"), st("sre_p"), st("sim_p"),
            st("lat_s"), st("kpe_s"), st("sre_s"), st("sim_s"))
```

```python
import functools
import math

import jax
import jax.numpy as jnp
import numpy as np
from jax import lax
from jax.experimental import pallas as pl
from jax.experimental.pallas import tpu as pltpu

F32 = jnp.float32
BF16 = jnp.bfloat16

ROPE_THETA = 10000.0
RMS_EPS = 1e-6
N_EXPERT_GROUPS = 8
TOPK_GROUPS = 4
TOP_K = 6
ROUTED_SCALE = 2.5
SSM_GROUP = 16

LANES = 128
SUBLANES = 8
VMEM_LIMIT_BYTES = 56 * 1024 * 1024

NEG_BIG = -1e30
TOP_K_PAD = 8
MOE_TOKENS = 256
PAGES_PER_STEP = 64
PAGES_PER_DOT = 16
NEW_KEY_ROWS = 16


def _cparams(*sem):
    return pltpu.CompilerParams(dimension_semantics=sem, vmem_limit_bytes=VMEM_LIMIT_BYTES)


def _dot(a, b):
    return jnp.dot(a, b, preferred_element_type=F32)


def _dot_nt(a, b):
    return lax.dot_general(a, b, (((1,), (1,)), ((), ())), preferred_element_type=F32)


def _rms(x, g):
    return x * lax.rsqrt(jnp.mean(x * x, axis=-1, keepdims=True) + RMS_EPS) * g


def _pack_bf16_pairs(x):
    bits = lax.bitcast_convert_type(x.astype(F32), jnp.uint32)
    n = x.shape[1] // 2
    return bits[:, :n] | (bits[:, n:] >> 16)


def _unpack_bf16_pairs(w):
    hi = lax.bitcast_convert_type(w & jnp.uint32(0xFFFF0000), F32).astype(BF16)
    lo = lax.bitcast_convert_type(w << 16, F32).astype(BF16)
    return hi, lo


def _swiglu_packed(xw, w1_ref, w3_ref, w2_ref):
    xa, xb = _unpack_bf16_pairs(xw)
    n = xw.shape[1]
    h1 = _dot(xa, w1_ref[:n, :]) + _dot(xb, w1_ref[n:, :])
    h3 = _dot(xa, w3_ref[:n, :]) + _dot(xb, w3_ref[n:, :])
    return _dot((h1 * jax.nn.sigmoid(h1) * h3).astype(BF16), w2_ref[...])


def _mod_kernel(c_ref, w_ref, b_ref, o_ref):
    c = c_ref[...]
    a = (c * jax.nn.sigmoid(c)).astype(BF16)
    o_ref[...] = _dot(a, w_ref[...]) + b_ref[...]


def _modulation(c_all, w_ada, b_ada):
    L, D, D6 = w_ada.shape
    R = c_all.shape[0]
    return pl.pallas_call(
        _mod_kernel,
        grid=(L, D6 // D),
        in_specs=[pl.BlockSpec((R, D), lambda l, j: (0, 0)),
                  pl.BlockSpec((None, D, D), lambda l, j: (l, 0, j)),
                  pl.BlockSpec((None, 1, D), lambda l, j: (l, 0, j))],
        out_specs=pl.BlockSpec((None, R, D), lambda l, j: (l, 0, j)),
        out_shape=jax.ShapeDtypeStruct((L, R, D6), F32),
        compiler_params=_cparams("arbitrary", "arbitrary"),
        name="adaln_mod",
    )(c_all, w_ada, b_ada)


def _in_proj_kernel(dims, x_ref, sh_ref, sc_ref, gmix_ref, wa_ref, wg_ref, gq_ref, wq_ref, wuk_ref,
                    sel_ref, gkv_ref, c8_ref, s8_ref, ct_ref, st_ref,
                    q_ref, kcat_ref, kv_ref, kpe_ref, u_ref, ga_ref, gs_ref):
    H, QL, KVL, ROPE, NOPE, SW, D = dims
    half = H * ROPE // 2
    x = x_ref[...]
    h = _rms(x, gmix_ref[...]) * (1.0 + sc_ref[...]) + sh_ref[...]
    hb = h.astype(BF16)
    za = _dot(hb, wa_ref[...])
    zg = _dot(hb, wg_ref[...])
    ga_ref[...] = jax.nn.sigmoid(zg[:, :D])
    gs_ref[...] = jax.nn.sigmoid(zg[:, D:])
    o0 = QL
    o1 = o0 + KVL
    o2 = o1 + LANES
    o3 = o2 + LANES
    u_ref[...] = za[:, o3:o3 + SW]
    kv = _rms(za[:, o0:o1], gkv_ref[...])
    kv_ref[...] = kv
    krot = za[:, o1:o2] * ct_ref[...] + za[:, o2:o3] * st_ref[...]
    kpe_ref[...] = krot[:, :ROPE]
    kcat_ref[:, :KVL] = kv.astype(BF16)
    kcat_ref[:, KVL:] = krot.astype(BF16)
    cqn = _rms(za[:, :QL], gq_ref[...]).astype(BF16)
    qq = _dot(cqn, wq_ref[...])
    qn = qq[:, :H * NOPE]
    x1 = qq[:, H * NOPE:H * NOPE + half]
    x2 = qq[:, H * NOPE + half:]
    c8 = c8_ref[...]
    s8 = s8_ref[...]
    rot = jnp.concatenate([x1 * c8 - x2 * s8, x2 * c8 + x1 * s8], axis=-1).astype(BF16)
    qpe = _dot(rot, sel_ref[...])
    heads_per_slab = LANES // NOPE
    for p in range(H // heads_per_slab):
        ql = _dot(qn[:, p * LANES:(p + 1) * LANES].astype(BF16), wuk_ref[p])
        for j in range(heads_per_slab):
            hh = p * heads_per_slab + j
            q_ref[hh, :, :KVL] = ql[:, j * KVL:(j + 1) * KVL].astype(BF16)
    for hh in range(H):
        q_ref[hh, :, KVL:] = qpe[:, hh * LANES:(hh + 1) * LANES].astype(BF16)


def _in_proj(x, mod, per_row_mod, p, tabs, dims, tm):
    H, QL, KVL, ROPE, NOPE, SW, D = dims
    NB, TT, _ = x.shape
    KC = KVL + LANES
    grid = (NB, TT // tm)
    if per_row_mod:
        mod_spec = lambda k: pl.BlockSpec((None, tm, D), lambda b, i, k=k: (b, i, k))
    else:
        mod_spec = lambda k: pl.BlockSpec((None, 1, D), lambda b, i, k=k: (b, 0, k))
    full = lambda a: pl.BlockSpec(a.shape, lambda b, i, n=a.ndim: (0,) * n)
    tab = lambda a: pl.BlockSpec((tm, a.shape[1]), lambda b, i: (i, 0))
    row = lambda w: pl.BlockSpec((None, tm, w), lambda b, i: (b, i, 0))
    c8, s8, ct, st = tabs
    weights = [p["g_mix"], p["w_a"], p["w_g"], p["g_qnorm"], p["w_q"], p["w_ukp"], p["sel"], p["g_kvnorm"]]
    in_specs = ([row(D), mod_spec(0), mod_spec(1)] + [full(w) for w in weights]
                + [tab(c8), tab(s8), tab(ct), tab(st)])
    out_shape = [jax.ShapeDtypeStruct((NB, H, TT, KC), BF16),
                 jax.ShapeDtypeStruct((NB, TT, KC), BF16),
                 jax.ShapeDtypeStruct((NB, TT, KVL), F32),
                 jax.ShapeDtypeStruct((NB, TT, ROPE), F32),
                 jax.ShapeDtypeStruct((NB, TT, SW), F32),
                 jax.ShapeDtypeStruct((NB, TT, D), F32),
                 jax.ShapeDtypeStruct((NB, TT, D), F32)]
    out_specs = [pl.BlockSpec((None, H, tm, KC), lambda b, i: (b, 0, i, 0)),
                 row(KC), row(KVL), row(ROPE), row(SW), row(D), row(D)]
    return pl.pallas_call(
        functools.partial(_in_proj_kernel, dims),
        grid=grid, in_specs=in_specs, out_specs=out_specs, out_shape=out_shape,
        compiler_params=_cparams("arbitrary", "arbitrary"),
        name="in_proj",
    )(x, mod, mod, *weights, c8, s8, ct, st)


def _attn_prompt_kernel(dims, qi_ref, kj_ref, q_ref, k_ref, o_ref, m_ref, l_ref, acc_ref):
    H, KVL, tq, ratio, scale = dims
    s_idx = pl.program_id(1)
    qi = qi_ref[s_idx]
    kj = kj_ref[s_idx]
    kj_last = qi // ratio
    M = H * tq

    @pl.when(kj == 0)
    def _():
        m_ref[...] = jnp.full(m_ref.shape, NEG_BIG, F32)
        l_ref[...] = jnp.zeros(l_ref.shape, F32)
        acc_ref[...] = jnp.zeros(acc_ref.shape, F32)

    q = q_ref[...].reshape(M, q_ref.shape[-1])
    k = k_ref[...]
    s = _dot_nt(q, k) * scale

    def update(s):
        m_prev = m_ref[...]
        m_new = jnp.maximum(m_prev, jnp.max(s, axis=-1, keepdims=True))
        alpha = jnp.exp(m_prev - m_new)
        pr = jnp.exp(s - m_new)
        l_ref[...] = alpha * l_ref[...] + jnp.sum(pr, axis=-1, keepdims=True)
        acc_ref[...] = alpha * acc_ref[...] + _dot(pr.astype(BF16), k[:, :KVL])
        m_ref[...] = m_new

    @pl.when(kj < kj_last)
    def _():
        update(s)

    @pl.when(kj == kj_last)
    def _():
        tk = k.shape[0]
        t_of_row = lax.broadcasted_iota(jnp.int32, (H, tq, tk), 1).reshape(M, tk)
        key = lax.broadcasted_iota(jnp.int32, (M, tk), 1)
        offset = (qi - kj * ratio) * tq
        update(jnp.where(key <= t_of_row + offset, s, NEG_BIG))
        out = acc_ref[...] / l_ref[...]
        for hh in range(H):
            o_ref[:, hh * KVL:(hh + 1) * KVL] = out[hh * tq:(hh + 1) * tq].astype(o_ref.dtype)


def _attn_prompt(q, kcat, KVL, scale, tq, tk):
    NB, H, TT, KC = q.shape
    nq = TT // tq
    ratio = tk // tq
    assert tk == ratio * tq and TT % tk == 0
    pairs = [(i, j) for i in range(nq) for j in range(i // ratio + 1)]
    qi = jnp.asarray([a for a, _ in pairs], jnp.int32)
    kj = jnp.asarray([b for _, b in pairs], jnp.int32)
    M = H * tq
    grid_spec = pltpu.PrefetchScalarGridSpec(
        num_scalar_prefetch=2,
        grid=(NB, len(pairs)),
        in_specs=[pl.BlockSpec((None, H, tq, KC), lambda b, s, qi, kj: (b, 0, qi[s], 0)),
                  pl.BlockSpec((None, tk, KC), lambda b, s, qi, kj: (b, kj[s], 0))],
        out_specs=pl.BlockSpec((None, tq, H * KVL), lambda b, s, qi, kj: (b, qi[s], 0)),
        scratch_shapes=[pltpu.VMEM((M, 1), F32), pltpu.VMEM((M, 1), F32), pltpu.VMEM((M, KVL), F32)],
    )
    return pl.pallas_call(
        functools.partial(_attn_prompt_kernel, (H, KVL, tq, ratio, scale)),
        grid_spec=grid_spec,
        out_shape=jax.ShapeDtypeStruct((NB, TT, H * KVL), BF16),
        compiler_params=_cparams("arbitrary", "arbitrary"),
        name="attn_prompt",
    )(qi, kj, q, kcat)


def _attn_sample_kernel(dims, pt_ref, q_ref, knew_ref, lat_hbm, krt_hbm, o_ref,
                        m_ref, l_ref, acc_ref, latbuf, krbuf, kb_ref, s_ref, sem):
    H, KVL, ROPE, TD, PGC, SUBP, PAGE, scale, layer = dims
    b = pl.program_id(0)
    c = pl.program_id(1)
    nc = pl.num_programs(1)
    step = b * nc + c
    n_steps = pl.num_programs(0) * nc
    slot = step % 2
    M = H * TD
    SUBK = SUBP * PAGE

    def issue(bb, cc, sl):
        for j in range(PGC):
            pg = pt_ref[bb, cc * PGC + j]
            pltpu.make_async_copy(lat_hbm.at[layer, pg], latbuf.at[sl, j], sem.at[0, sl]).start()
            pltpu.make_async_copy(krt_hbm.at[layer, pg], krbuf.at[sl, j], sem.at[1, sl]).start()

    @pl.when(step == 0)
    def _():
        issue(0, 0, 0)

    @pl.when(step + 1 < n_steps)
    def _():
        wrap = c + 1 == nc
        issue(jnp.where(wrap, b + 1, b), jnp.where(wrap, 0, c + 1), 1 - slot)

    @pl.when(c == 0)
    def _():
        m_ref[...] = jnp.full(m_ref.shape, NEG_BIG, F32)
        l_ref[...] = jnp.zeros(l_ref.shape, F32)
        acc_ref[...] = jnp.zeros(acc_ref.shape, F32)

    pltpu.make_async_copy(lat_hbm.at[layer, pl.ds(0, PGC)], latbuf.at[slot], sem.at[0, slot]).wait()
    pltpu.make_async_copy(krt_hbm.at[layer, pl.ds(0, PGC)], krbuf.at[slot], sem.at[1, slot]).wait()

    q = q_ref[...]
    ql = q[:, :KVL]
    qp = q[:, KVL:KVL + ROPE]

    def update(s, pv_fn):
        m_prev = m_ref[...]
        m_new = jnp.maximum(m_prev, jnp.max(s, axis=-1, keepdims=True))
        alpha = jnp.exp(m_prev - m_new)
        pr = jnp.exp(s - m_new)
        l_ref[...] = alpha * l_ref[...] + jnp.sum(pr, axis=-1, keepdims=True)
        acc_ref[...] = alpha * acc_ref[...] + pv_fn(pr.astype(BF16))
        m_ref[...] = m_new

    for sb in range(PGC // SUBP):
        for j in range(SUBP):
            pgi = sb * SUBP + j
            kb_ref[pgi * PAGE:(pgi + 1) * PAGE, :] = latbuf[slot, pgi].astype(BF16)
        krt = jnp.concatenate([krbuf[slot, sb * SUBP + j].astype(BF16) for j in range(SUBP)], axis=1)
        s_ref[:, sb * SUBK:(sb + 1) * SUBK] = (
            _dot_nt(ql, kb_ref[sb * SUBK:(sb + 1) * SUBK, :]) + _dot(qp, krt)) * scale

    def pv_past(pb):
        out = _dot(pb[:, :SUBK], kb_ref[:SUBK, :])
        for sb in range(1, PGC // SUBP):
            out = out + _dot(pb[:, sb * SUBK:(sb + 1) * SUBK], kb_ref[sb * SUBK:(sb + 1) * SUBK, :])
        return out

    update(s_ref[...], pv_past)

    @pl.when(c == nc - 1)
    def _():
        kn = knew_ref[...]
        KN = kn.shape[0]
        sn = _dot_nt(q, kn) * scale
        t_of_row = lax.broadcasted_iota(jnp.int32, (H, TD, KN), 1).reshape(M, KN)
        key = lax.broadcasted_iota(jnp.int32, (M, KN), 1)
        update(jnp.where(key <= t_of_row, sn, NEG_BIG), lambda pb: _dot(pb, kn[:, :KVL]))
        o_ref[...] = (acc_ref[...] / l_ref[...]).astype(o_ref.dtype)


def _attn_sample(q, knew, cache_latent, cache_krope_t, page_table, layer, scale, H, TD):
    NB, M, KC = q.shape
    _, _, PAGE, KVL = cache_latent.shape
    ROPE = cache_krope_t.shape[2]
    n_pages = page_table.shape[1]
    PGC = min(PAGES_PER_STEP, n_pages)
    SUBP = min(PAGES_PER_DOT, PGC)
    assert n_pages % PGC == 0 and PGC % SUBP == 0
    KN = knew.shape[1]
    grid_spec = pltpu.PrefetchScalarGridSpec(
        num_scalar_prefetch=1, grid=(NB, n_pages // PGC),
        in_specs=[pl.BlockSpec((None, M, KC), lambda b, c, pt: (b, 0, 0)),
                  pl.BlockSpec((None, KN, KC), lambda b, c, pt: (b, 0, 0)),
                  pl.BlockSpec(memory_space=pl.ANY), pl.BlockSpec(memory_space=pl.ANY)],
        out_specs=pl.BlockSpec((None, M, KVL), lambda b, c, pt: (b, 0, 0)),
        scratch_shapes=[pltpu.VMEM((M, 1), F32), pltpu.VMEM((M, 1), F32), pltpu.VMEM((M, KVL), F32),
                        pltpu.VMEM((2, PGC, PAGE, KVL), F32), pltpu.VMEM((2, PGC, ROPE, PAGE), F32),
                        pltpu.VMEM((PGC * PAGE, KVL), BF16), pltpu.VMEM((M, PGC * PAGE), F32),
                        pltpu.SemaphoreType.DMA((2, 2))],
    )
    return pl.pallas_call(
        functools.partial(_attn_sample_kernel, (H, KVL, ROPE, TD, PGC, SUBP, PAGE, scale, layer)),
        grid_spec=grid_spec,
        out_shape=jax.ShapeDtypeStruct((NB, M, KVL), BF16),
        compiler_params=_cparams("arbitrary", "arbitrary"),
        name="attn_sample",
    )(page_table, q, knew, cache_latent, cache_krope_t)


def _s5_kernel(dims, u_ref, h0r_ref, h0i_ref, ar_ref, ai_ref, bre_ref, bim_ref, cre_ref, cim_ref,
               d_ref, wglu_ref, bglu_ref, o_ref, fr_ref, fi_ref, sr_ref, si_ref, hr_ref, hi_ref):
    NB, tt, SW, NS, CW, RG = dims
    i = pl.program_id(0)

    @pl.when(i == 0)
    def _():
        sr_ref[...] = h0r_ref[...]
        si_ref[...] = h0i_ref[...]

    u = u_ref[...]
    ub = u.astype(BF16)
    nb_cols = bre_ref.shape[2]
    for n in range(NS // nb_cols):
        k0 = (n * nb_cols // (NS // SW)) // LANES * LANES
        uu = ub[:, k0:k0 + LANES]
        hr_ref[:, n * nb_cols:(n + 1) * nb_cols] = _dot(uu, bre_ref[n])
        hi_ref[:, n * nb_cols:(n + 1) * nb_cols] = _dot(uu, bim_ref[n])

    for cc in range(NS // CW):
        cols = slice(cc * CW, (cc + 1) * CW)
        ar = ar_ref[:, cols]
        ai = ai_ref[:, cols]
        for rg in range(NB // RG):
            rows = slice(rg * RG, (rg + 1) * RG)

            def body(t, carry):
                h_re, h_im = carry
                r0 = pl.multiple_of(t * NB + rg * RG, SUBLANES)
                n_re = ar * h_re - ai * h_im + hr_ref[pl.ds(r0, RG), cols]
                n_im = ar * h_im + ai * h_re + hi_ref[pl.ds(r0, RG), cols]
                hr_ref[pl.ds(r0, RG), cols] = n_re
                hi_ref[pl.ds(r0, RG), cols] = n_im
                return n_re, n_im

            h_re, h_im = lax.fori_loop(0, tt, body, (sr_ref[rows, cols], si_ref[rows, cols]),
                                       unroll=min(tt, 4))
            sr_ref[rows, cols] = h_re
            si_ref[rows, cols] = h_im

    kc = cre_ref.shape[1]
    ys = []
    for j in range(SW // cre_ref.shape[2]):
        hb_re = hr_ref[:, j * kc:(j + 1) * kc].astype(BF16)
        hb_im = hi_ref[:, j * kc:(j + 1) * kc].astype(BF16)
        ys.append(_dot(hb_re, cre_ref[j]) - _dot(hb_im, cim_ref[j]))
    y = jnp.concatenate(ys, axis=-1) + d_ref[...] * u
    g = jax.nn.gelu(y)
    o_ref[...] = (g * jax.nn.sigmoid(_dot(g.astype(BF16), wglu_ref[...]) + bglu_ref[...])).astype(o_ref.dtype)

    @pl.when(i == pl.num_programs(0) - 1)
    def _():
        fr_ref[...] = sr_ref[...]
        fi_ref[...] = si_ref[...]


def _s5(u_tm, h0_re, h0_im, p, NB, tt):
    R, SW = u_tm.shape
    NS = h0_re.shape[1]
    rows = tt * NB
    CW = 512
    RG = 16
    assert NB % RG == 0 and NS % CW == 0 and R % rows == 0
    full = lambda a: pl.BlockSpec(a.shape, lambda i, n=a.ndim: (0,) * n)
    ws = [p["abar_re"], p["abar_im"], p["b_re"], p["b_im"], p["c_re"], p["c_im"], p["ssm_d"], p["w_glu"], p["b_glu"]]
    return pl.pallas_call(
        functools.partial(_s5_kernel, (NB, tt, SW, NS, CW, RG)),
        grid=(R // rows,),
        in_specs=[pl.BlockSpec((rows, SW), lambda i: (i, 0)), full(h0_re), full(h0_im)] + [full(w) for w in ws],
        out_specs=[pl.BlockSpec((rows, SW), lambda i: (i, 0)),
                   pl.BlockSpec((NB, NS), lambda i: (0, 0)), pl.BlockSpec((NB, NS), lambda i: (0, 0))],
        out_shape=[jax.ShapeDtypeStruct((R, SW), BF16),
                   jax.ShapeDtypeStruct((NB, NS), F32), jax.ShapeDtypeStruct((NB, NS), F32)],
        scratch_shapes=[pltpu.VMEM((NB, NS), F32), pltpu.VMEM((NB, NS), F32),
                        pltpu.VMEM((rows, NS), F32), pltpu.VMEM((rows, NS), F32)],
        compiler_params=_cparams("arbitrary"),
        name="s5_scan",
    )(u_tm, h0_re, h0_im, *ws)


def _merge_kernel(dims, x_ref, ol_ref, os_ref, ga_ref, gs_ref, gta_ref, shm_ref, scm_ref,
                  wuv_ref, wap_ref, wsp_ref, wout_ref, gffn_ref, wrt_ref,
                  xo_ref, h2_ref, lg_ref):
    n_slab, slab_k = dims
    ol = ol_ref[...]
    o = jnp.concatenate([_dot(ol[:, p * slab_k:(p + 1) * slab_k], wuv_ref[p]) for p in range(n_slab)], axis=-1)
    a = _dot(o.astype(BF16), wap_ref[...])
    s = _dot(os_ref[...], wsp_ref[...])
    merged = (ga_ref[...] * a + gs_ref[...] * s).astype(BF16)
    m = _dot(merged, wout_ref[...])
    x = x_ref[...] + gta_ref[...] * m
    xo_ref[...] = x
    h2 = (_rms(x, gffn_ref[...]) * (1.0 + scm_ref[...]) + shm_ref[...]).astype(BF16)
    h2_ref[...] = _pack_bf16_pairs(h2)
    lg_ref[...] = _dot_nt(wrt_ref[...], h2)


def _merge(x, o_lat, o_s, ga, gs, mod, per_row_mod, p, tm):
    NB, TT, D = x.shape
    HK = o_lat.shape[-1]
    SW = o_s.shape[-1]
    E = p["w_router_t"].shape[0]
    n_slab = p["w_uvp"].shape[0]
    slab_k = p["w_uvp"].shape[1]
    if per_row_mod:
        mod_spec = lambda k: pl.BlockSpec((None, tm, D), lambda b, i, k=k: (b, i, k))
    else:
        mod_spec = lambda k: pl.BlockSpec((None, 1, D), lambda b, i, k=k: (b, 0, k))
    row = lambda w: pl.BlockSpec((None, tm, w), lambda b, i: (b, i, 0))
    full = lambda a: pl.BlockSpec(a.shape, lambda b, i, n=a.ndim: (0,) * n)
    ws = [p["w_uvp"], p["w_attn_proj"], p["w_ssm_proj"], p["w_out"], p["g_ffn"], p["w_router_t"]]
    return pl.pallas_call(
        functools.partial(_merge_kernel, (n_slab, slab_k)),
        grid=(NB, TT // tm),
        in_specs=[row(D), row(HK), row(SW), row(D), row(D), mod_spec(2), mod_spec(3), mod_spec(4)]
                 + [full(w) for w in ws],
        out_specs=[row(D), row(D // 2), pl.BlockSpec((None, E, tm), lambda b, i: (b, 0, i))],
        out_shape=[jax.ShapeDtypeStruct((NB, TT, D), F32), jax.ShapeDtypeStruct((NB, TT, D // 2), jnp.uint32),
                   jax.ShapeDtypeStruct((NB, E, TT), F32)],
        compiler_params=_cparams("arbitrary", "arbitrary"),
        name="merge_out_proj",
    )(x, o_lat, o_s, ga, gs, mod, mod, mod, *ws)


def _first_index(mask, iota, big):
    return jnp.min(jnp.where(mask, iota, big), axis=0, keepdims=True)


def _route_kernel(dims, lg_ref, bias_ref, tri_ref, idx_ref, wts_ref, rank_ref, cnt_ref, base_ref):
    E, NG = dims
    GS = E // NG
    first = (pl.program_id(0) == 0) & (pl.program_id(1) == 0)

    @pl.when(first)
    def _():
        base_ref[...] = jnp.zeros(base_ref.shape, F32)

    scores = jax.nn.sigmoid(lg_ref[...])
    choice = scores + bias_ref[...]
    tm = scores.shape[1]
    eio = lax.broadcasted_iota(jnp.int32, (E, tm), 0).astype(F32)
    gio = lax.broadcasted_iota(jnp.int32, (GS, tm), 0).astype(F32)
    gscore = []
    for g in range(NG):
        cg = choice[g * GS:(g + 1) * GS]
        m1 = jnp.max(cg, axis=0, keepdims=True)
        i1 = _first_index(cg == m1, gio, float(GS))
        m2 = jnp.max(jnp.where(gio == i1, -jnp.inf, cg), axis=0, keepdims=True)
        gscore.append(m1 + m2)
    gscore = jnp.concatenate(gscore, axis=0)
    nio = lax.broadcasted_iota(jnp.int32, (NG, tm), 0).astype(F32)
    gsel = jnp.zeros((NG, tm), F32)
    rem = gscore
    for _ in range(TOPK_GROUPS):
        mg = jnp.max(rem, axis=0, keepdims=True)
        ig = _first_index(rem == mg, nio, float(NG))
        hit = nio == ig
        gsel = jnp.where(hit, 1.0, gsel)
        rem = jnp.where(hit, -jnp.inf, rem)
    allowed = jnp.concatenate(
        [jnp.broadcast_to(gsel[g:g + 1], (GS, tm)) for g in range(NG)], axis=0) > 0.5
    rem = jnp.where(allowed, choice, -jnp.inf)
    picks, vals = [], []
    chosen = jnp.zeros((E, tm), F32)
    for _ in range(TOP_K):
        mk = jnp.max(rem, axis=0, keepdims=True)
        ik = _first_index(rem == mk, eio, float(E))
        hit = eio == ik
        picks.append(ik)
        vals.append(jnp.sum(jnp.where(hit, scores, 0.0), axis=0, keepdims=True))
        chosen = chosen + hit.astype(F32)
        rem = jnp.where(hit, -jnp.inf, rem)
    total = vals[0]
    for v in vals[1:]:
        total = total + v
    prefix = _dot(chosen.astype(BF16), tri_ref[...]) + base_ref[...]
    ranks = [jnp.sum(jnp.where(eio == ik, prefix, 0.0), axis=0, keepdims=True) for ik in picks]
    base_ref[...] = base_ref[...] + jnp.sum(chosen, axis=1, keepdims=True)
    pad_i = [jnp.zeros((1, tm), jnp.int32)] * (TOP_K_PAD - TOP_K)
    pad_f = [jnp.zeros((1, tm), F32)] * (TOP_K_PAD - TOP_K)
    idx_ref[...] = jnp.concatenate([ik.astype(jnp.int32) for ik in picks] + pad_i, axis=0)
    wts_ref[...] = jnp.concatenate([v / total * ROUTED_SCALE for v in vals] + pad_f, axis=0)
    rank_ref[...] = jnp.concatenate([r.astype(jnp.int32) for r in ranks] + pad_i, axis=0)
    cnt_ref[...] = jnp.broadcast_to(base_ref[...], cnt_ref.shape)


def _route(logits_t, bias, tri, tm):
    NB, E, TT = logits_t.shape
    blk = lambda: pl.BlockSpec((None, TOP_K_PAD, tm), lambda b, i: (b, 0, i))
    return pl.pallas_call(
        functools.partial(_route_kernel, (E, N_EXPERT_GROUPS)),
        grid=(NB, TT // tm),
        in_specs=[pl.BlockSpec((None, E, tm), lambda b, i: (b, 0, i)),
                  pl.BlockSpec((E, 1), lambda b, i: (0, 0)),
                  pl.BlockSpec((tm, tm), lambda b, i: (0, 0))],
        out_specs=[blk(), blk(), blk(), pl.BlockSpec((E, LANES), lambda b, i: (0, 0))],
        out_shape=[jax.ShapeDtypeStruct((NB, TOP_K_PAD, TT), jnp.int32),
                   jax.ShapeDtypeStruct((NB, TOP_K_PAD, TT), F32),
                   jax.ShapeDtypeStruct((NB, TOP_K_PAD, TT), jnp.int32),
                   jax.ShapeDtypeStruct((E, LANES), F32)],
        scratch_shapes=[pltpu.VMEM((E, 1), F32)],
        compiler_params=_cparams("arbitrary", "arbitrary"),
        name="route_topk",
    )(logits_t, bias, tri)


def _gather_rows(idx_hbm, src_hbm, idx_smem, buf, isem, rsem, step, n_steps, rows):
    slot = step % 2
    nslot = 1 - slot

    def idx_copy(s, sl):
        return pltpu.make_async_copy(idx_hbm.at[s], idx_smem.at[sl], isem.at[sl])

    def issue(sl):
        for r in range(rows):
            pltpu.make_async_copy(src_hbm.at[idx_smem[sl, r]], buf.at[sl, r], rsem.at[sl]).start(priority=r % 2)

    @pl.when(step == 0)
    def _():
        c = idx_copy(0, 0)
        c.start()
        c.wait()
        issue(0)

        @pl.when(n_steps > 1)
        def _():
            idx_copy(1, 1).start()

    @pl.when(step + 1 < n_steps)
    def _():
        idx_copy(step + 1, nslot).wait()
        issue(nslot)

    pltpu.make_async_copy(src_hbm.at[pl.ds(0, rows)], buf.at[slot], rsem.at[slot]).wait()

    @pl.when(step + 2 < n_steps)
    def _():
        idx_copy(step + 2, slot).start()

    return slot


def _dispatch_kernel(dims, zb_ref, nvb_ref, dest_hbm, x_ref, xs_hbm, idx_smem, zbuf, isem, rsem, zsem):
    tm, E, rows, n_blocks = dims
    step = pl.program_id(0)
    n_steps = pl.num_programs(0)
    slot = step % 2

    def idx_copy(s, sl):
        return pltpu.make_async_copy(dest_hbm.at[s], idx_smem.at[sl], isem.at[sl])

    def zero_copy(blk):
        return pltpu.make_async_copy(zbuf, xs_hbm.at[pl.ds(pl.multiple_of(blk * rows, rows), rows)], zsem)

    @pl.when(step == 0)
    def _():
        idx_copy(0, 0).start()
        zbuf[...] = jnp.zeros(zbuf.shape, zbuf.dtype)
        for e in range(E):
            @pl.when(zb_ref[e] >= 0)
            def _():
                zero_copy(zb_ref[e]).start()
        lax.fori_loop(nvb_ref[0], n_blocks, lambda i, c: (zero_copy(i).start(), c)[1], 0)
        for e in range(E):
            @pl.when(zb_ref[e] >= 0)
            def _():
                zero_copy(zb_ref[e]).wait()
        lax.fori_loop(nvb_ref[0], n_blocks, lambda i, c: (zero_copy(i).wait(), c)[1], 0)

    idx_copy(step, slot).wait()

    @pl.when(step + 1 < n_steps)
    def _():
        idx_copy(step + 1, 1 - slot).start()

    def row_copy(r):
        return pltpu.make_async_copy(x_ref.at[r % tm], xs_hbm.at[idx_smem[slot, r]], rsem)

    for r in range(TOP_K * tm):
        row_copy(r).start(priority=r % 2)
    for k in range(TOP_K):
        pltpu.make_async_copy(x_ref, xs_hbm.at[pl.ds(0, tm)], rsem).wait()


def _dispatch(last_block, n_valid, dest, h2p, n_blocks, rows):
    n_steps, per_step = dest.shape
    tm = per_step // TOP_K
    N, W = h2p.shape
    E = last_block.shape[0]
    grid_spec = pltpu.PrefetchScalarGridSpec(
        num_scalar_prefetch=2, grid=(n_steps,),
        in_specs=[pl.BlockSpec(memory_space=pl.ANY), pl.BlockSpec((tm, W), lambda i, zb, nv: (i, 0))],
        out_specs=pl.BlockSpec(memory_space=pl.ANY),
        scratch_shapes=[pltpu.SMEM((2, per_step), jnp.int32), pltpu.VMEM((rows, W), jnp.uint32),
                        pltpu.SemaphoreType.DMA((2,)), pltpu.SemaphoreType.DMA(()), pltpu.SemaphoreType.DMA(())],
    )
    return pl.pallas_call(
        functools.partial(_dispatch_kernel, (tm, E, rows, n_blocks)), grid_spec=grid_spec,
        out_shape=jax.ShapeDtypeStruct((n_blocks * rows, W), jnp.uint32),
        compiler_params=_cparams("arbitrary"),
        name="moe_dispatch",
    )(last_block, n_valid, dest, h2p)


def _experts_kernel(be_ref, nvb_ref, x_ref, w1_ref, w3_ref, w2_ref, y_ref):
    @pl.when(pl.program_id(0) < nvb_ref[0])
    def _():
        y_ref[...] = _swiglu_packed(x_ref[...], w1_ref, w3_ref, w2_ref)

    @pl.when(pl.program_id(0) >= nvb_ref[0])
    def _():
        y_ref[...] = jnp.zeros(y_ref.shape, y_ref.dtype)


def _experts(block_e, n_valid, xs, w1, w3, w2, rows):
    n_blocks = block_e.shape[0]
    W = xs.shape[1]
    _, D, F = w1.shape
    blk = lambda i, be, nv: jnp.minimum(i, nv[0] - 1)
    grid_spec = pltpu.PrefetchScalarGridSpec(
        num_scalar_prefetch=2, grid=(n_blocks,),
        in_specs=[pl.BlockSpec((rows, W), lambda i, be, nv: (blk(i, be, nv), 0)),
                  pl.BlockSpec((None, D, F), lambda i, be, nv: (be[blk(i, be, nv)], 0, 0)),
                  pl.BlockSpec((None, D, F), lambda i, be, nv: (be[blk(i, be, nv)], 0, 0)),
                  pl.BlockSpec((None, F, D), lambda i, be, nv: (be[blk(i, be, nv)], 0, 0))],
        out_specs=pl.BlockSpec((rows, D), lambda i, be, nv: (i, 0)),
    )
    return pl.pallas_call(
        _experts_kernel, grid_spec=grid_spec,
        out_shape=jax.ShapeDtypeStruct((n_blocks * rows, D), F32),
        compiler_params=_cparams("arbitrary"),
        name="routed_experts",
    )(block_e, n_valid, xs, w1, w3, w2)


def _combine_kernel(dims, pos_hbm, y_hbm, x_ref, h2_ref, wts_ref, gtm_ref, w1_ref, w3_ref, w2_ref, gfin_ref,
                    xo_ref, idx_smem, ybuf, isem, rsem):
    tm, final = dims
    step = pl.program_id(0)
    slot = _gather_rows(pos_hbm, y_hbm, idx_smem, ybuf, isem, rsem, step, pl.num_programs(0), TOP_K * tm)
    wts = wts_ref[...]
    routed = wts[:, 0:1] * ybuf[slot, 0:tm]
    for k in range(1, TOP_K):
        routed = routed + wts[:, k:k + 1] * ybuf[slot, k * tm:(k + 1) * tm]
    shared = _swiglu_packed(h2_ref[...], w1_ref, w3_ref, w2_ref)
    x = x_ref[...] + gtm_ref[...] * (routed + shared)
    xo_ref[...] = _rms(x, gfin_ref[...]) if final else x


def _combine(pos, y_sorted, x, h2, wts, gtm, per_row_mod, p, g_final, final, tm, tokens_per_mod):
    N, D = x.shape
    n_steps = N // tm
    if per_row_mod:
        gt_spec = pl.BlockSpec((None, tm, D), lambda i: (0, i, 5))
    else:
        per = tokens_per_mod // tm
        gt_spec = pl.BlockSpec((None, 1, D), lambda i: (i // per, 0, 5))
    full = lambda a: pl.BlockSpec(a.shape, lambda i, n=a.ndim: (0,) * n)
    row = lambda w: pl.BlockSpec((tm, w), lambda i: (i, 0))
    return pl.pallas_call(
        functools.partial(_combine_kernel, (tm, final)),
        grid=(n_steps,),
        in_specs=[pl.BlockSpec(memory_space=pl.ANY), pl.BlockSpec(memory_space=pl.ANY),
                  row(D), row(D // 2), row(TOP_K_PAD), gt_spec,
                  full(p["w_s1"]), full(p["w_s3"]), full(p["w_s2"]), full(g_final)],
        out_specs=row(D),
        out_shape=jax.ShapeDtypeStruct((N, D), F32),
        scratch_shapes=[pltpu.SMEM((2, TOP_K * tm), jnp.int32), pltpu.VMEM((2, TOP_K * tm, D), F32),
                        pltpu.SemaphoreType.DMA((2,)), pltpu.SemaphoreType.DMA((2,))],
        compiler_params=_cparams("arbitrary"),
        name="combine_shared",
    )(pos, y_sorted, x, h2, wts, gtm, p["w_s1"], p["w_s3"], p["w_s2"], g_final)


def _block_diag_groups(blocks, group):
    n, a, b = blocks.shape
    eye = jnp.eye(group, dtype=blocks.dtype)
    b4 = blocks.reshape(n // group, group, a, b)
    return (eye[None, :, None, :, None] * b4[:, :, :, None, :]).reshape(n // group, group * a, group * b)


def _prep_layer(l, w, cfg):
    H, QL, KVL, ROPE, NOPE, SW, D, G, P, C = cfg
    half = ROPE // 2
    w_in = w["w_in"][l]
    o_kpe = QL + KVL
    o_u = o_kpe + ROPE
    o_ga = o_u + SW
    k1 = w_in[:, o_kpe:o_kpe + half]
    k2 = w_in[:, o_kpe + half:o_kpe + ROPE]
    zpad = jnp.zeros((D, LANES - ROPE), F32)
    w_a = jnp.concatenate([w_in[:, :o_kpe], k1, k2, zpad, k2, k1, zpad, w_in[:, o_u:o_ga]], axis=1)
    w_uq = w["w_uq"][l]
    w_q = jnp.concatenate([w_uq[:, :, :NOPE].reshape(QL, H * NOPE),
                           w_uq[:, :, NOPE:NOPE + half].reshape(QL, H * half),
                           w_uq[:, :, NOPE + half:].reshape(QL, H * half)], axis=1)
    hps = LANES // NOPE
    ukt = jnp.swapaxes(w["w_uk"][l], 1, 2)
    w_ukp = _block_diag_groups(ukt, hps)
    uv = w["w_uv"][l]
    hpv = LANES // uv.shape[-1]
    w_uvp = _block_diag_groups(uv, hpv)
    a_re, a_im = w["ssm_a_re"][l], w["ssm_a_im"][l]
    dt = jnp.exp(w["ssm_log_dt"][l])[:, None]
    mag = jnp.exp(dt * a_re)
    abar_re, abar_im = mag * jnp.cos(dt * a_im), mag * jnp.sin(dt * a_im)
    den = a_re * a_re + a_im * a_im
    coef_re = ((abar_re - 1.0) * a_re + abar_im * a_im) / den
    coef_im = (abar_im * a_re - (abar_re - 1.0) * a_im) / den
    b_re, b_im = w["ssm_b_re"][l], w["ssm_b_im"][l]
    bbar_re = coef_re[..., None] * b_re - coef_im[..., None] * b_im
    bbar_im = coef_re[..., None] * b_im + coef_im[..., None] * b_re
    NS = G * P
    NBC = 2 * LANES
    gpb = NBC // P

    def in_blocks(bb):
        full = _block_diag_groups(jnp.swapaxes(bb, 1, 2), G)[0]
        slabs = full.reshape(SW // LANES, LANES, NS // NBC, NBC)
        n = np.arange(NS // NBC)
        return slabs[n * gpb * C // LANES, :, n, :].astype(BF16)

    OC = 2 * LANES
    gpo = OC // C

    def out_blocks(cc):
        full = _block_diag_groups(jnp.swapaxes(cc, 1, 2), G)[0]
        n = np.arange(SW // OC)
        return full.reshape(SW // OC, gpo * P, SW // OC, OC)[n, :, n, :].astype(BF16)

    return dict(
        g_mix=w["g_mix"][l][None], w_a=w_a.astype(BF16), w_g=w_in[:, o_ga:].astype(BF16),
        g_qnorm=w["g_qnorm"][l][None], w_q=w_q.astype(BF16), w_ukp=w_ukp.astype(BF16),
        g_kvnorm=w["g_kvnorm"][l][None], w_uvp=w_uvp.astype(BF16),
        abar_re=abar_re.reshape(1, NS), abar_im=abar_im.reshape(1, NS),
        b_re=in_blocks(bbar_re), b_im=in_blocks(bbar_im),
        c_re=out_blocks(w["ssm_c_re"][l]), c_im=out_blocks(w["ssm_c_im"][l]),
        ssm_d=w["ssm_d"][l].reshape(1, SW), w_glu=w["w_glu"][l].astype(BF16), b_glu=w["b_glu"][l][None],
        w_attn_proj=w["w_attn_proj"][l].astype(BF16), w_ssm_proj=w["w_ssm_proj"][l].astype(BF16),
        w_out=w["w_out"][l].astype(BF16), g_ffn=w["g_ffn"][l][None],
        w_router_t=w["w_router"][l].T.astype(BF16), router_bias=w["router_bias"][l][:, None],
        w_e1=w["w_e1"][l].astype(BF16), w_e3=w["w_e3"][l].astype(BF16), w_e2=w["w_e2"][l].astype(BF16),
        w_s1=w["w_s1"][l].astype(BF16), w_s3=w["w_s3"][l].astype(BF16), w_s2=w["w_s2"][l].astype(BF16),
    )


def _rope_tabs(pos, H, ROPE):
    half = ROPE // 2
    inv = ROPE_THETA ** (-jnp.arange(0, ROPE, 2, dtype=F32) / ROPE)
    ang = pos.astype(F32)[:, None] * inv[None, :]
    cos, sin = jnp.cos(ang), jnp.sin(ang)
    T = pos.shape[0]
    zpad = jnp.zeros((T, LANES - ROPE), F32)
    return (jnp.tile(cos, (1, H)), jnp.tile(sin, (1, H)),
            jnp.concatenate([cos, cos, zpad], axis=1), jnp.concatenate([-sin, sin, zpad], axis=1))


def _head_select(H, ROPE):
    half = ROPE // 2
    sel = np.zeros((2 * H * half, H * LANES), np.float32)
    for hh in range(H):
        for i in range(half):
            sel[hh * half + i, hh * LANES + i] = 1.0
            sel[H * half + hh * half + i, hh * LANES + half + i] = 1.0
    return jnp.asarray(sel, BF16)


def _moe(x2, h2p, logits_t, gtm, per_row_mod, tokens_per_mod, p, tri, g_final, final, tm_route, rows):
    N, D = x2.shape
    NB, E, TT = logits_t.shape
    tm = min(MOE_TOKENS, N)
    idx_t, wts_t, rank_t, cnt = _route(logits_t, p["router_bias"], tri, tm_route)
    to_rows = lambda a: jnp.swapaxes(a, 1, 2).reshape(N, TOP_K_PAD)
    idx, wts, rank = to_rows(idx_t), to_rows(wts_t), to_rows(rank_t)
    counts = cnt[:, 0].astype(jnp.int32)
    padded = (counts + rows - 1) // rows * rows
    pad_end = jnp.cumsum(padded)
    pad_start = pad_end - padded
    n_blocks = -(-(N * TOP_K) // rows) + E
    dest = pad_start[idx[:, :TOP_K]] + rank[:, :TOP_K]
    pos = jnp.swapaxes(dest.reshape(N // tm, tm, TOP_K), 1, 2).reshape(N // tm, TOP_K * tm)
    block_start = jnp.arange(n_blocks, dtype=jnp.int32) * rows
    block_e = jnp.minimum(jnp.sum((pad_end[None, :] <= block_start[:, None]).astype(jnp.int32), axis=1), E - 1)
    n_valid = (pad_end[E - 1:] // rows).astype(jnp.int32)
    last_block = jnp.where(padded > 0, pad_end // rows - 1, -1).astype(jnp.int32)
    xs = _dispatch(last_block, n_valid, pos, h2p, n_blocks, rows)
    y_sorted = _experts(block_e, n_valid, xs, p["w_e1"], p["w_e3"], p["w_e2"], rows)
    return _combine(pos, y_sorted, x2, h2p, wts, gtm, per_row_mod, p, g_final, final, tm, tokens_per_mod)


def kernel(x_prompt, x_sample, cache_latent, cache_krope, state_ssm_re, state_ssm_im, page_table, c_prompt, c_sample, w_ada, b_ada, g_mix, w_in, g_qnorm, w_uq, g_kvnorm, w_uk, w_uv, ssm_a_re, ssm_a_im, ssm_log_dt, ssm_b_re, ssm_b_im, ssm_c_re, ssm_c_im, ssm_d, w_glu, b_glu, w_attn_proj, w_ssm_proj, w_out, g_ffn, w_router, router_bias, w_e1, w_e3, w_e2, w_s1, w_s3, w_s2, g_final):
    w = dict(w_in=w_in, g_mix=g_mix, g_qnorm=g_qnorm, w_uq=w_uq, g_kvnorm=g_kvnorm, w_uk=w_uk, w_uv=w_uv,
             ssm_a_re=ssm_a_re, ssm_a_im=ssm_a_im, ssm_log_dt=ssm_log_dt, ssm_b_re=ssm_b_re, ssm_b_im=ssm_b_im,
             ssm_c_re=ssm_c_re, ssm_c_im=ssm_c_im, ssm_d=ssm_d, w_glu=w_glu, b_glu=b_glu,
             w_attn_proj=w_attn_proj, w_ssm_proj=w_ssm_proj, w_out=w_out, g_ffn=g_ffn, w_router=w_router,
             router_bias=router_bias, w_e1=w_e1, w_e3=w_e3, w_e2=w_e2, w_s1=w_s1, w_s3=w_s3, w_s2=w_s2)
    BP, TP, D = x_prompt.shape
    BS, TS, _ = x_sample.shape
    L = w_in.shape[0]
    QL, H, QKD = w_uq.shape[1:]
    KVL = cache_latent.shape[-1]
    ROPE = cache_krope.shape[-1]
    PAGE = cache_latent.shape[2]
    NOPE = QKD - ROPE
    G, P = ssm_a_re.shape[1:]
    C = SSM_GROUP
    SW = G * C
    NS = G * P
    E = w_router.shape[-1]
    past_len = page_table.shape[1] * PAGE
    scale = float(QKD) ** -0.5
    cfg = (H, QL, KVL, ROPE, NOPE, SW, D, G, P, C)
    dims = (H, QL, KVL, ROPE, NOPE, SW, D)
    assert H * ROPE // 2 == LANES and LANES % NOPE == 0 and KVL % LANES == 0

    NSAMP = BS * TS
    tm_p = min(256, TP)
    tm_s = min(256, NSAMP)
    tq = min(256, TP)
    tk = min(2 * tq, TP)
    tt_p = max(1, min(TP, 512 // BP))
    rows_for = lambda n: int(min(512, max(128, pl.next_power_of_2(n * TOP_K // E))))
    rows_p, rows_s = rows_for(BP * TP), rows_for(NSAMP)
    cache_krope_t = jnp.swapaxes(cache_krope, 2, 3)

    mod_all = _modulation(jnp.concatenate([c_prompt, c_sample], axis=0), w_ada.astype(BF16), b_ada[:, None, :])
    tabs_p = _rope_tabs(jnp.arange(TP, dtype=jnp.int32), H, ROPE)
    tabs_s1 = _rope_tabs(past_len + jnp.arange(TS, dtype=jnp.int32), H, ROPE)
    tabs_s = tuple(jnp.tile(t, (BS, 1)) for t in tabs_s1)
    sel = _head_select(H, ROPE)
    tri_p = jnp.asarray(np.triu(np.ones((tm_p, tm_p), np.float32), 1), BF16)
    tri_s = jnp.asarray(np.triu(np.ones((tm_s, tm_s), np.float32), 1), BF16)
    g_fin = g_final[None]
    zeros_state = jnp.zeros((BP, NS), F32)

    xp = x_prompt
    xs = x_sample.reshape(1, NSAMP, D)
    outs = {k: [] for k in ("lat_p", "kpe_p", "sre_p", "sim_p", "lat_s", "kpe_s", "sre_s", "sim_s")}
    for l in range(L):
        p = _prep_layer(l, w, cfg)
        p["sel"] = sel
        final = l == L - 1
        mod_p = mod_all[l, :BP][:, None, :]
        mod_s = jnp.repeat(mod_all[l, BP:], TS, axis=0)[None]

        q, kcat, kv, kpe, u, ga, gs = _in_proj(xp, mod_p, False, p, tabs_p, dims, tm_p)
        outs["lat_p"].append(kv)
        outs["kpe_p"].append(kpe)
        o_lat = _attn_prompt(q, kcat, KVL, scale, tq, tk)
        u_tm = jnp.swapaxes(u, 0, 1).reshape(TP * BP, SW)
        o_s_tm, f_re, f_im = _s5(u_tm, zeros_state, zeros_state, p, BP, tt_p)
        outs["sre_p"].append(f_re.reshape(BP, G, P))
        outs["sim_p"].append(f_im.reshape(BP, G, P))
        o_s = jnp.swapaxes(o_s_tm.reshape(TP, BP, SW), 0, 1)
        x1, h2, lg = _merge(xp, o_lat, o_s, ga, gs, mod_p, False, p, tm_p)
        xp = _moe(x1.reshape(BP * TP, D), h2.reshape(BP * TP, D // 2), lg, mod_p, False, TP, p, tri_p,
                  g_fin, final, tm_p, rows_p).reshape(BP, TP, D)

        q, kcat, kv, kpe, u, ga, gs = _in_proj(xs, mod_s, True, p, tabs_s, dims, tm_s)
        outs["lat_s"].append(kv.reshape(BS, TS, KVL))
        outs["kpe_s"].append(kpe.reshape(BS, TS, ROPE))
        KC = KVL + LANES
        q_s = jnp.swapaxes(q.reshape(H, BS, TS, KC), 0, 1).reshape(BS, H * TS, KC)
        knew = jnp.pad(kcat.reshape(BS, TS, KC), ((0, 0), (0, NEW_KEY_ROWS - TS), (0, 0)))
        o_lat_s = _attn_sample(q_s, knew, cache_latent, cache_krope_t, page_table, l, scale, H, TS)
        o_lat_s = jnp.swapaxes(o_lat_s.reshape(BS, H, TS, KVL), 1, 2).reshape(1, NSAMP, H * KVL)
        u_tm = jnp.swapaxes(u.reshape(BS, TS, SW), 0, 1).reshape(TS * BS, SW)
        o_s_tm, f_re, f_im = _s5(u_tm, state_ssm_re[l].reshape(BS, NS), state_ssm_im[l].reshape(BS, NS), p, BS, TS)
        outs["sre_s"].append(f_re.reshape(BS, G, P))
        outs["sim_s"].append(f_im.reshape(BS, G, P))
        o_s = jnp.swapaxes(o_s_tm.reshape(TS, BS, SW), 0, 1).reshape(1, NSAMP, SW)
        x1, h2, lg = _merge(xs, o_lat_s, o_s, ga, gs, mod_s, True, p, tm_s)
        xs = _moe(x1.reshape(NSAMP, D), h2.reshape(NSAMP, D // 2), lg, mod_s, True, NSAMP, p, tri_s,
                  g_fin, final, tm_s, rows_s).reshape(1, NSAMP, D)

    st = lambda k: jnp.stack(outs[k])
    return (xp, xs.reshape(BS, TS, D), st("lat_p"), st("kpe_p"), st("sre_p"), st("sim_p"),
            st("lat_s"), st("kpe_s"), st("sre_s"), st("sim_s"))
```
